```python
import jax, jax.numpy as jnp
from jax import lax
import numpy as np

D_MODEL = 1024
BATCH = 2
SEQ = 8192
DEPTH = 1

D_CONV = D_MODEL
CONV_WIDTH = 3
HEAD_DIM = 64
HEADS_PER_GROUP = 8
GROUPS = ((128, 1), (512, 4), (2048, 16))
N_GROUPS = len(GROUPS)
N_ATT_HEADS = N_GROUPS * HEADS_PER_GROUP
ATT_GROUP_W = HEADS_PER_GROUP * HEAD_DIM
ATT_QKV_W = N_GROUPS * ATT_GROUP_W
D_FF = 2816
LN_EPS = 1e-5
ALPHA = (2.0 * DEPTH) ** 0.25
BETA = (8.0 * DEPTH) ** -0.25
MASK_VALUE = -1e30

OFF_B = 0
OFF_C = OFF_B + D_CONV
OFF_H = OFF_C + D_CONV
OFF_Q = OFF_H + D_CONV
OFF_K = OFF_Q + ATT_QKV_W
OFF_V = OFF_K + ATT_QKV_W
OFF_GA = OFF_V + ATT_QKV_W
OFF_GB = OFF_GA + D_MODEL
N_IN = OFF_GB + D_MODEL

kernel_name = "hybrid_shortconv_dilated_alibi_deepnorm_encoder"


def layer_norm(x, g, b):
    xf = x.astype(jnp.float32)
    mu = jnp.mean(xf, -1, keepdims=True)
    xc = xf - mu
    var = jnp.mean(xc * xc, -1, keepdims=True)
    return (xc * lax.rsqrt(var + LN_EPS) * g + b).astype(x.dtype)


def dwconv3(u, w):
    up = jnp.pad(u, ((0, 0), (1, 1), (0, 0)))
    return up[:, :-2] * w[0] + up[:, 1:-1] * w[1] + up[:, 2:] * w[2]


def alibi_slopes(n):
    return jnp.exp2(-8.0 * jnp.arange(1, n + 1, dtype=jnp.float32) / n)


def dilated_window_attention(q, k, v, dil, radius, slopes):
    bsz, seq, nh, hd = q.shape
    sub_len = seq // dil
    blk = radius
    nb = -(-sub_len // blk)
    lp = nb * blk

    def to_sub(a):
        a = a.reshape(bsz, sub_len, dil, nh, hd).transpose(0, 2, 1, 3, 4)
        return a.reshape(bsz * dil, sub_len, nh, hd)

    qs, ks, vs = to_sub(q), to_sub(k), to_sub(v)
    qb = jnp.pad(qs, ((0, 0), (0, lp - sub_len), (0, 0), (0, 0))).reshape(bsz * dil, nb, blk, nh, hd)

    def windows(a):
        ap = jnp.pad(a, ((0, 0), (blk, lp - sub_len + blk), (0, 0), (0, 0)))
        ap = ap.reshape(bsz * dil, nb + 2, blk, nh, hd)
        return jnp.concatenate([ap[:, :-2], ap[:, 1:-1], ap[:, 2:]], axis=2)

    kw, vw = windows(ks), windows(vs)
    qpos = jnp.arange(lp).reshape(nb, blk)
    kpos = (jnp.arange(nb)[:, None] - 1) * blk + jnp.arange(3 * blk)[None, :]
    rel = kpos[:, None, :] - qpos[:, :, None]
    valid = (jnp.abs(rel) <= radius) & (kpos[:, None, :] >= 0) & (kpos[:, None, :] < sub_len)
    dist = (jnp.abs(rel) * dil).astype(jnp.float32)
    bias = -slopes[None, :, None, None] * dist[:, None]
    s = jnp.einsum('bnqhd,bnkhd->bnhqk', qb, kw).astype(jnp.float32) * (hd ** -0.5) + bias
    s = jnp.where(valid[:, None], s, MASK_VALUE)
    m = jnp.max(s, -1, keepdims=True)
    p = jnp.exp(s - m)
    den = jnp.sum(p, -1, keepdims=True)
    o = jnp.einsum('bnhqk,bnkhd->bnqhd', (p / den).astype(v.dtype), vw)
    lse = jnp.transpose((m + jnp.log(den))[..., 0], (0, 1, 3, 2))
    o = o.reshape(bsz, dil, lp, nh, hd)[:, :, :sub_len].transpose(0, 2, 1, 3, 4).reshape(bsz, seq, nh, hd)
    lse = lse.reshape(bsz, dil, lp, nh)[:, :, :sub_len].transpose(0, 2, 1, 3).reshape(bsz, seq, nh)
    return o, lse


def setup_inputs(seed: int = 0) -> dict:
    key = jax.random.key(seed)
    ks = jax.random.split(key, 24)

    def nrm(k, shape, scale):
        return jax.random.normal(k, shape, jnp.float32) * scale

    col_scale = np.ones((N_IN,), np.float32)
    col_scale[OFF_H:OFF_H + D_CONV] = BETA
    col_scale[OFF_V:OFF_V + ATT_QKV_W] = BETA
    L = DEPTH
    return {
        "x": nrm(ks[0], (BATCH, SEQ, D_MODEL), 1.0),
        "ln0_g": 1.0 + nrm(ks[1], (D_MODEL,), 0.02),
        "ln0_b": nrm(ks[2], (D_MODEL,), 0.02),
        "w_in": nrm(ks[3], (L, D_MODEL, N_IN), D_MODEL ** -0.5) * jnp.asarray(col_scale),
        "b_in": nrm(ks[4], (L, N_IN), 0.02),
        "conv_w": nrm(ks[5], (L, CONV_WIDTH, D_CONV), CONV_WIDTH ** -0.5),
        "w_a": nrm(ks[6], (L, D_CONV, D_MODEL), BETA * D_CONV ** -0.5),
        "w_b": nrm(ks[7], (L, ATT_GROUP_W, D_MODEL), BETA * ATT_GROUP_W ** -0.5),
        "w_o": nrm(ks[8], (L, D_MODEL, D_MODEL), BETA * D_MODEL ** -0.5),
        "b_o": nrm(ks[9], (L, D_MODEL), 0.02),
        "ln1_g": 1.0 + nrm(ks[10], (L, D_MODEL), 0.02),
        "ln1_b": nrm(ks[11], (L, D_MODEL), 0.02),
        "w_up": nrm(ks[12], (L, D_MODEL, 2 * D_FF), BETA * D_MODEL ** -0.5),
        "b_up": nrm(ks[13], (L, 2 * D_FF), 0.02),
        "ffn_conv_w": nrm(ks[14], (L, CONV_WIDTH, D_FF), CONV_WIDTH ** -0.5),
        "ffn_conv_b": nrm(ks[15], (L, D_FF), 0.02),
        "w_down": nrm(ks[16], (L, D_FF, D_MODEL), BETA * D_FF ** -0.5),
        "b_down": nrm(ks[17], (L, D_MODEL), 0.02),
        "ln2_g": 1.0 + nrm(ks[18], (L, D_MODEL), 0.02),
        "ln2_b": nrm(ks[19], (L, D_MODEL), 0.02),
    }


def reference(x, ln0_g, ln0_b, w_in, b_in, conv_w, w_a, w_b, w_o, b_o, ln1_g, ln1_b,
              w_up, b_up, ffn_conv_w, ffn_conv_b, w_down, b_down, ln2_g, ln2_b):
    bsz, seq, _ = x.shape
    slopes = alibi_slopes(N_ATT_HEADS).reshape(N_GROUPS, HEADS_PER_GROUP)
    h = layer_norm(x, ln0_g, ln0_b)
    for l in range(DEPTH):
        proj = h @ w_in[l] + b_in[l]
        gate_b = proj[..., OFF_B:OFF_B + D_CONV]
        gate_c = proj[..., OFF_C:OFF_C + D_CONV]
        hin = proj[..., OFF_H:OFF_H + D_CONV]
        y_a = (gate_b * dwconv3(gate_c * hin, conv_w[l])) @ w_a[l]
        q = proj[..., OFF_Q:OFF_Q + ATT_QKV_W].reshape(bsz, seq, N_GROUPS, HEADS_PER_GROUP, HEAD_DIM)
        k = proj[..., OFF_K:OFF_K + ATT_QKV_W].reshape(bsz, seq, N_GROUPS, HEADS_PER_GROUP, HEAD_DIM)
        v = proj[..., OFF_V:OFF_V + ATT_QKV_W].reshape(bsz, seq, N_GROUPS, HEADS_PER_GROUP, HEAD_DIM)
        outs, lses = [], []
        for g, (window, dil) in enumerate(GROUPS):
            o, lse = dilated_window_attention(q[:, :, g], k[:, :, g], v[:, :, g], dil,
                                              window // (2 * dil), slopes[g])
            outs.append(o)
            lses.append(lse)
        wts = jax.nn.softmax(jnp.stack(lses, 0), axis=0)
        comb = jnp.sum(wts[..., None].astype(x.dtype) * jnp.stack(outs, 0), axis=0)
        y_b = comb.reshape(bsz, seq, ATT_GROUP_W) @ w_b[l]
        g_a = jax.nn.sigmoid(proj[..., OFF_GA:OFF_GA + D_MODEL])
        g_b = jax.nn.sigmoid(proj[..., OFF_GB:OFF_GB + D_MODEL])
        mix = (g_a * y_a + g_b * y_b) @ w_o[l] + b_o[l]
        h = layer_norm(ALPHA * h + mix, ln1_g[l], ln1_b[l])
        up = h @ w_up[l] + b_up[l]
        a, gte = up[..., :D_FF], up[..., D_FF:]
        f = jax.nn.gelu(dwconv3(a, ffn_conv_w[l]) + ffn_conv_b[l], approximate=False) * gte
        ffn = f @ w_down[l] + b_down[l]
        h = layer_norm(ALPHA * h + ffn, ln2_g[l], ln2_b[l])
    return h
```

```python
import functools

import numpy as np
import jax
import jax.numpy as jnp
from jax import lax
from jax.experimental import pallas as pl
from jax.experimental.pallas import tpu as pltpu

D_MODEL = 1024
D_CONV = D_MODEL
HEAD_DIM = 64
HEADS_PER_GROUP = 8
GROUPS = ((128, 1), (512, 4), (2048, 16))
N_GROUPS = len(GROUPS)
N_ATT_HEADS = N_GROUPS * HEADS_PER_GROUP
ATT_GROUP_W = HEADS_PER_GROUP * HEAD_DIM
ATT_QKV_W = N_GROUPS * ATT_GROUP_W
D_FF = 2816
LN_EPS = 1e-5
DEPTH = 1
assert DEPTH == 1
ALPHA = (2.0 * DEPTH) ** 0.25
MASK_VALUE = -1e30

OFF_B = 0
OFF_C = OFF_B + D_CONV
OFF_H = OFF_C + D_CONV
OFF_Q = OFF_H + D_CONV
OFF_K = OFF_Q + ATT_QKV_W
OFF_V = OFF_K + ATT_QKV_W
OFF_GA = OFF_V + ATT_QKV_W
OFF_GB = OFF_GA + D_MODEL

RADIUS = 64
QBLK = RADIUS
KBLK = 3 * RADIUS
BF16_ROWS = 16
F32_ROWS = 8
VMEM_LIMIT = 56 * 1024 * 1024

assert all(w // (2 * d) == RADIUS for w, d in GROUPS)


def _layer_norm(xf, g, b):
    mu = jnp.mean(xf, -1, keepdims=True)
    xc = xf - mu
    var = jnp.mean(xc * xc, -1, keepdims=True)
    return xc * lax.rsqrt(var + LN_EPS) * g + b


def _alibi_slopes():
    n = N_ATT_HEADS
    return np.exp2(-8.0 * np.arange(1, n + 1, dtype=np.float64) / n).reshape(N_GROUPS, HEADS_PER_GROUP)


def _ln_proj_kernel(x_ref, g_ref, b_ref, w_ref, bias_ref, o_ref, hb_ref):
    @pl.when(pl.program_id(3) == 0)
    def _():
        hb_ref[...] = _layer_norm(x_ref[...], g_ref[...], b_ref[...]).astype(jnp.bfloat16)

    acc = jnp.dot(hb_ref[...], w_ref[...], preferred_element_type=jnp.float32)
    o_ref[...] = (acc + bias_ref[...]).astype(o_ref.dtype)


def _ln_proj(x, ln_g, ln_b, w, bias, *, dil, tm, tn, name):
    bsz, seq, dm = x.shape
    sub = seq // dil
    n = w.shape[1]
    xv = x.reshape(bsz, sub, dil * dm)
    return pl.pallas_call(
        _ln_proj_kernel,
        grid=(bsz, dil, sub // tm, n // tn),
        in_specs=[
            pl.BlockSpec((None, tm, dm), lambda b, r, i, j: (b, i, r)),
            pl.BlockSpec((1, dm), lambda b, r, i, j: (0, 0)),
            pl.BlockSpec((1, dm), lambda b, r, i, j: (0, 0)),
            pl.BlockSpec((dm, tn), lambda b, r, i, j: (0, j)),
            pl.BlockSpec((1, tn), lambda b, r, i, j: (0, j)),
        ],
        out_specs=pl.BlockSpec((None, None, tm, tn), lambda b, r, i, j: (b, r, i, j)),
        out_shape=jax.ShapeDtypeStruct((bsz, dil, sub, n), jnp.bfloat16),
        scratch_shapes=[pltpu.VMEM((tm, dm), jnp.bfloat16)],
        compiler_params=pltpu.CompilerParams(
            dimension_semantics=("parallel", "parallel", "parallel", "arbitrary"),
            vmem_limit_bytes=VMEM_LIMIT),
        name=name,
    )(xv, ln_g, ln_b, w, bias)


def _attn_kernel(q_ref, kp_ref, kc_ref, kn_ref, vp_ref, vc_ref, vn_ref, o_ref, st_ref,
                 kbuf, vbuf, *, dil, slopes, sub_len, tq):
    i = pl.program_id(2)
    kbuf[0:QBLK, :] = kp_ref[...]
    kbuf[QBLK:QBLK + tq, :] = kc_ref[...]
    kbuf[QBLK + tq:, :] = kn_ref[...]
    vbuf[0:QBLK, :] = vp_ref[...]
    vbuf[QBLK:QBLK + tq, :] = vc_ref[...]
    vbuf[QBLK + tq:, :] = vn_ref[...]

    row = lax.broadcasted_iota(jnp.int32, (QBLK, KBLK), 0)
    col = lax.broadcasted_iota(jnp.int32, (QBLK, KBLK), 1)
    rel = jnp.abs(col - QBLK - row)
    band = rel <= RADIUS
    dist = (rel * dil).astype(jnp.float32)

    def body(j, carry):
        r0 = pl.multiple_of(j * QBLK, QBLK)
        kpos = col + (i * tq + j * QBLK - QBLK)
        valid = band & (kpos >= 0) & (kpos < sub_len)
        qb = q_ref[pl.ds(r0, QBLK), :]
        kb = kbuf[pl.ds(r0, KBLK), :]
        vb = vbuf[pl.ds(r0, KBLK), :]
        outs, stats = [], []
        for h in range(HEADS_PER_GROUP):
            sl = slice(h * HEAD_DIM, (h + 1) * HEAD_DIM)
            s = lax.dot_general(qb[:, sl], kb[:, sl], (((1,), (1,)), ((), ())),
                                preferred_element_type=jnp.float32)
            s = s - slopes[h] * dist
            s = jnp.where(valid, s, MASK_VALUE)
            m = jnp.max(s, -1, keepdims=True)
            p = jnp.exp(s - m)
            den = jnp.sum(p, -1, keepdims=True)
            o = jnp.dot(p.astype(jnp.bfloat16), vb[:, sl], preferred_element_type=jnp.float32) / den
            outs.append(o)
            stats.append(jnp.broadcast_to(m + jnp.log(den), (QBLK, HEAD_DIM)))
        o_ref[pl.ds(r0, QBLK), :] = jnp.concatenate(outs, -1).astype(o_ref.dtype)
        st_ref[pl.ds(r0, QBLK), :] = jnp.concatenate(stats, -1)
        return carry

    lax.fori_loop(0, tq // QBLK, body, 0)


def _attention(qkv, *, dil, slopes, tq, name):
    bsz, _, sub, _ = qkv.shape
    w = ATT_GROUP_W
    nq = tq // QBLK
    nblk = sub // QBLK
    cur = lambda c: pl.BlockSpec((None, None, tq, w), lambda b, r, i: (b, r, i, c))
    prev = lambda c: pl.BlockSpec((None, None, QBLK, w),
                                  lambda b, r, i: (b, r, jnp.maximum(i * nq - 1, 0), c))
    nxt = lambda c: pl.BlockSpec((None, None, QBLK, w),
                                 lambda b, r, i: (b, r, jnp.minimum((i + 1) * nq, nblk - 1), c))
    out_spec = pl.BlockSpec((None, tq, w), lambda b, r, i: (b, i, r))
    o, st = pl.pallas_call(
        functools.partial(_attn_kernel, dil=dil, slopes=tuple(float(s) for s in slopes),
                          sub_len=sub, tq=tq),
        grid=(bsz, dil, sub // tq),
        in_specs=[cur(0), prev(1), cur(1), nxt(1), prev(2), cur(2), nxt(2)],
        out_specs=[out_spec, out_spec],
        out_shape=[jax.ShapeDtypeStruct((bsz, sub, dil * w), jnp.bfloat16),
                   jax.ShapeDtypeStruct((bsz, sub, dil * w), jnp.float32)],
        scratch_shapes=[pltpu.VMEM((tq + 2 * QBLK, w), jnp.bfloat16),
                        pltpu.VMEM((tq + 2 * QBLK, w), jnp.bfloat16)],
        compiler_params=pltpu.CompilerParams(
            dimension_semantics=("parallel", "parallel", "parallel"),
            vmem_limit_bytes=VMEM_LIMIT),
        name=name,
    )(qkv, qkv, qkv, qkv, qkv, qkv, qkv)
    return o.reshape(bsz, sub * dil, w), st.reshape(bsz, sub * dil, w)


def _dwconv3_rows(u, prev_row, next_row, w_ref):
    n = u.shape[0]
    row = lax.broadcasted_iota(jnp.int32, u.shape, 0)
    u_dn = jnp.where(row == 0, prev_row, pltpu.roll(u, 1, axis=0))
    u_up = jnp.where(row == n - 1, next_row, pltpu.roll(u, n - 1, axis=0))
    return u_dn * w_ref[0:1, :] + u * w_ref[1:2, :] + u_up * w_ref[2:3, :]


def _mix_kernel(x_ref, gb_ref, gc_ref, hn_ref, ga_ref, gbb_ref,
                gcp_ref, gcn_ref, hnp_ref, hnn_ref,
                o0_ref, o1_ref, o2_ref, s0_ref, s1_ref, s2_ref,
                cw_ref, wa_ref, wb_ref, wo_ref, bo_ref,
                g0_ref, b0_ref, g1_ref, b1_ref, out_ref):
    i = pl.program_id(1)
    last = pl.num_programs(1) - 1
    f32 = jnp.float32
    h = _layer_norm(x_ref[...], g0_ref[...], b0_ref[...])

    u = gc_ref[...].astype(f32) * hn_ref[...].astype(f32)
    u_prev = (gcp_ref[...].astype(f32) * hnp_ref[...].astype(f32))[BF16_ROWS - 1:BF16_ROWS, :]
    u_next = (gcn_ref[...].astype(f32) * hnn_ref[...].astype(f32))[0:1, :]
    u_prev = jnp.where(i > 0, u_prev, 0.0)
    u_next = jnp.where(i < last, u_next, 0.0)
    conv = _dwconv3_rows(u, u_prev, u_next, cw_ref)
    y_a = jnp.dot((gb_ref[...].astype(f32) * conv).astype(jnp.bfloat16), wa_ref[...],
                  preferred_element_type=f32)

    s0, s1, s2 = s0_ref[...], s1_ref[...], s2_ref[...]
    m = jnp.maximum(jnp.maximum(s0, s1), s2)
    e0, e1, e2 = jnp.exp(s0 - m), jnp.exp(s1 - m), jnp.exp(s2 - m)
    comb = (e0 * o0_ref[...].astype(f32) + e1 * o1_ref[...].astype(f32)
            + e2 * o2_ref[...].astype(f32)) / (e0 + e1 + e2)
    y_b = jnp.dot(comb.astype(jnp.bfloat16), wb_ref[...], preferred_element_type=f32)

    merged = (jax.nn.sigmoid(ga_ref[...].astype(f32)) * y_a
              + jax.nn.sigmoid(gbb_ref[...].astype(f32)) * y_b)
    mix = jnp.dot(merged.astype(jnp.bfloat16), wo_ref[...], preferred_element_type=f32) + bo_ref[...]
    out_ref[...] = _layer_norm(ALPHA * h + mix, g1_ref[...], b1_ref[...])


def _mix(x, pm, outs, stats, cw, wa, wb, wo, bo, g0, b0, g1, b1, *, tm):
    bsz, seq, dm = x.shape
    nhalo = tm // BF16_ROWS
    nrow_blk = seq // BF16_ROWS
    col = lambda c: pl.BlockSpec((None, tm, dm), lambda b, i: (b, i, c))
    prev = lambda c: pl.BlockSpec((None, BF16_ROWS, dm),
                                  lambda b, i: (b, jnp.maximum(i * nhalo - 1, 0), c))
    nxt = lambda c: pl.BlockSpec((None, BF16_ROWS, dm),
                                 lambda b, i: (b, jnp.minimum((i + 1) * nhalo, nrow_blk - 1), c))
    att = pl.BlockSpec((None, tm, ATT_GROUP_W), lambda b, i: (b, i, 0))
    full = lambda a: pl.BlockSpec(a.shape, lambda b, i: (0,) * a.ndim)
    weights = (cw, wa, wb, wo, bo, g0, b0, g1, b1)
    return pl.pallas_call(
        _mix_kernel,
        grid=(bsz, seq // tm),
        in_specs=[col(0), col(0), col(1), col(2), col(3), col(4),
                  prev(1), nxt(1), prev(2), nxt(2)]
                 + [att] * 6 + [full(a) for a in weights],
        out_specs=pl.BlockSpec((None, tm, dm), lambda b, i: (b, i, 0)),
        out_shape=jax.ShapeDtypeStruct((bsz, seq, dm), jnp.float32),
        compiler_params=pltpu.CompilerParams(
            dimension_semantics=("parallel", "parallel"),
            vmem_limit_bytes=VMEM_LIMIT),
        name="mix_merge",
    )(x, pm, pm, pm, pm, pm, pm, pm, pm, pm, *outs, *stats, *weights)


def _ffn_kernel(h_ref, hp_ref, hn_ref, wua_ref, wug_ref, bua_ref, bug_ref, cw_ref, cb_ref,
                wd_ref, bd_ref, g_ref, b_ref, out_ref, lhs_ref, acc_ref, *, tm):
    i = pl.program_id(1)
    c = pl.program_id(2)
    last = pl.num_programs(1) - 1
    f32 = jnp.float32

    @pl.when(c == 0)
    def _():
        lhs_ref[0:tm, :] = h_ref[...].astype(jnp.bfloat16)
        halo = jnp.concatenate([hp_ref[...], hn_ref[...]], axis=0)
        lhs_ref[tm:, :] = halo.astype(jnp.bfloat16)
        acc_ref[...] = jnp.zeros_like(acc_ref)

    a_all = jnp.dot(lhs_ref[...], wua_ref[...], preferred_element_type=f32) + bua_ref[...]
    gate = jnp.dot(lhs_ref[0:tm, :], wug_ref[...], preferred_element_type=f32) + bug_ref[...]
    a = a_all[0:tm, :]
    a_prev = jnp.where(i > 0, a_all[tm + F32_ROWS - 1:tm + F32_ROWS, :], 0.0)
    a_next = jnp.where(i < last, a_all[tm + F32_ROWS:tm + F32_ROWS + 1, :], 0.0)
    pre = _dwconv3_rows(a, a_prev, a_next, cw_ref) + cb_ref[...]
    f = 0.5 * pre * (1.0 + lax.erf(pre * (2.0 ** -0.5))) * gate
    acc_ref[...] += jnp.dot(f.astype(jnp.bfloat16), wd_ref[...], preferred_element_type=f32)

    @pl.when(c == pl.num_programs(2) - 1)
    def _():
        out_ref[...] = _layer_norm(ALPHA * h_ref[...] + acc_ref[...] + bd_ref[...],
                                   g_ref[...], b_ref[...])


def _ffn(h1, wu, bu, cw, cb, wd, bd, g, b, *, tm, ck):
    bsz, seq, dm = h1.shape
    nch = D_FF // ck
    nhalo = tm // F32_ROWS
    nrow_blk = seq // F32_ROWS
    const = lambda a: pl.BlockSpec(a.shape, lambda bb, i, c: (0,) * a.ndim)
    return pl.pallas_call(
        functools.partial(_ffn_kernel, tm=tm),
        grid=(bsz, seq // tm, nch),
        in_specs=[
            pl.BlockSpec((None, tm, dm), lambda bb, i, c: (bb, i, 0)),
            pl.BlockSpec((None, F32_ROWS, dm), lambda bb, i, c: (bb, jnp.maximum(i * nhalo - 1, 0), 0)),
            pl.BlockSpec((None, F32_ROWS, dm),
                         lambda bb, i, c: (bb, jnp.minimum((i + 1) * nhalo, nrow_blk - 1), 0)),
            pl.BlockSpec((dm, ck), lambda bb, i, c: (0, c)),
            pl.BlockSpec((dm, ck), lambda bb, i, c: (0, nch + c)),
            pl.BlockSpec((1, ck), lambda bb, i, c: (0, c)),
            pl.BlockSpec((1, ck), lambda bb, i, c: (0, nch + c)),
            pl.BlockSpec((3, ck), lambda bb, i, c: (0, c)),
            pl.BlockSpec((1, ck), lambda bb, i, c: (0, c)),
            pl.BlockSpec((ck, dm), lambda bb, i, c: (c, 0)),
            const(bd), const(g), const(b),
        ],
        out_specs=pl.BlockSpec((None, tm, dm), lambda bb, i, c: (bb, i, 0)),
        out_shape=jax.ShapeDtypeStruct((bsz, seq, dm), jnp.float32),
        scratch_shapes=[pltpu.VMEM((tm + 2 * F32_ROWS, dm), jnp.bfloat16),
                        pltpu.VMEM((tm, dm), jnp.float32)],
        compiler_params=pltpu.CompilerParams(
            dimension_semantics=("parallel", "parallel", "arbitrary"),
            vmem_limit_bytes=VMEM_LIMIT),
        name="ffn",
    )(h1, h1, h1, wu, wu, bu, bu, cw, cb, wd, bd, g, b)


def kernel(x, ln0_g, ln0_b, w_in, b_in, conv_w, w_a, w_b, w_o, b_o, ln1_g, ln1_b,
           w_up, b_up, ffn_conv_w, ffn_conv_b, w_down, b_down, ln2_g, ln2_b):
    bf16 = jnp.bfloat16
    row = lambda v: v.reshape(1, -1)
    slopes = _alibi_slopes()
    g0, b0 = row(ln0_g), row(ln0_b)
    bsz, seq, _ = x.shape
    l = 0
    wi, bi = w_in[l], b_in[l]
    w_main = jnp.concatenate([wi[:, OFF_B:OFF_Q], wi[:, OFF_GA:]], axis=1).astype(bf16)
    b_main = jnp.concatenate([bi[OFF_B:OFF_Q], bi[OFF_GA:]])
    pm = _ln_proj(x, g0, b0, w_main, row(b_main), dil=1, tm=1024, tn=1024,
                  name="proj_main").reshape(bsz, seq, -1)
    outs, stats = [], []
    for g, (_, dil) in enumerate(GROUPS):
        lo, hi = g * ATT_GROUP_W, (g + 1) * ATT_GROUP_W
        scale = HEAD_DIM ** -0.5
        w_qkv = jnp.concatenate([wi[:, OFF_Q + lo:OFF_Q + hi] * scale, wi[:, OFF_K + lo:OFF_K + hi],
                                 wi[:, OFF_V + lo:OFF_V + hi]], axis=1).astype(bf16)
        b_qkv = jnp.concatenate([bi[OFF_Q + lo:OFF_Q + hi] * scale, bi[OFF_K + lo:OFF_K + hi],
                                 bi[OFF_V + lo:OFF_V + hi]])
        qkv = _ln_proj(x, g0, b0, w_qkv, row(b_qkv), dil=dil, tm=512, tn=3 * ATT_GROUP_W,
                       name=f"proj_qkv{g}")
        o, st = _attention(qkv, dil=dil, slopes=slopes[g], tq=512, name=f"attn{g}")
        outs.append(o)
        stats.append(st)
    h1 = _mix(x, pm, outs, stats, conv_w[l], w_a[l].astype(bf16), w_b[l].astype(bf16),
              w_o[l].astype(bf16), row(b_o[l]), g0, b0, row(ln1_g[l]), row(ln1_b[l]), tm=512)
    return _ffn(h1, w_up[l].astype(bf16), row(b_up[l]), ffn_conv_w[l], row(ffn_conv_b[l]),
                w_down[l].astype(bf16), row(b_down[l]), row(ln2_g[l]), row(ln2_b[l]),
                tm=512, ck=D_FF // 2)
```

```python
import functools

import numpy as np
import jax
import jax.numpy as jnp
from jax import lax
from jax.experimental import pallas as pl
from jax.experimental.pallas import tpu as pltpu

D_MODEL = 1024
D_CONV = D_MODEL
HEAD_DIM = 64
HEADS_PER_GROUP = 8
GROUPS = ((128, 1), (512, 4), (2048, 16))
N_GROUPS = len(GROUPS)
N_ATT_HEADS = N_GROUPS * HEADS_PER_GROUP
ATT_GROUP_W = HEADS_PER_GROUP * HEAD_DIM
ATT_QKV_W = N_GROUPS * ATT_GROUP_W
D_FF = 2816
LN_EPS = 1e-5
DEPTH = 1
assert DEPTH == 1
ALPHA = (2.0 * DEPTH) ** 0.25
MASK_VALUE = -1e30

OFF_B = 0
OFF_C = OFF_B + D_CONV
OFF_H = OFF_C + D_CONV
OFF_Q = OFF_H + D_CONV
OFF_K = OFF_Q + ATT_QKV_W
OFF_V = OFF_K + ATT_QKV_W
OFF_GA = OFF_V + ATT_QKV_W
OFF_GB = OFF_GA + D_MODEL

RADIUS = 64
QBLK = 2 * RADIUS
KBLK = QBLK + 2 * RADIUS
LANES = 128
BF16_ROWS = 16
F32_ROWS = 8
HEADS_PER_DOT = 4
DOT_W = HEADS_PER_DOT * HEAD_DIM
VMEM_LIMIT = 56 * 1024 * 1024

assert all(w // (2 * d) == RADIUS for w, d in GROUPS)


def _layer_norm(xf, g, b):
    mu = jnp.mean(xf, -1, keepdims=True)
    xc = xf - mu
    var = jnp.mean(xc * xc, -1, keepdims=True)
    return xc * lax.rsqrt(var + LN_EPS) * g + b


def _alibi_slopes():
    n = N_ATT_HEADS
    return np.exp2(-8.0 * np.arange(1, n + 1, dtype=np.float64) / n).reshape(N_GROUPS, HEADS_PER_GROUP)


def _ln_proj_kernel(x_ref, g_ref, b_ref, w_ref, bias_ref, o_ref, hb_ref, slab_ref, *, dil, tm):
    n = tm // dil

    @pl.when(pl.program_id(2) == 0)
    def _():
        h = _layer_norm(x_ref[...], g_ref[...], b_ref[...])
        if dil == 1:
            hb_ref[...] = h.astype(jnp.bfloat16)
        else:
            for c in range(D_MODEL // LANES):
                slab_ref[c] = h[:, c * LANES:(c + 1) * LANES]
            for r in range(dil):
                for c in range(D_MODEL // LANES):
                    hb_ref[r * n:(r + 1) * n, c * LANES:(c + 1) * LANES] = (
                        slab_ref[c, pl.ds(r, n, stride=dil), :].astype(jnp.bfloat16))

    acc = jnp.dot(hb_ref[...], w_ref[...], preferred_element_type=jnp.float32) + bias_ref[...]
    for r in range(dil):
        o_ref[r] = acc[r * n:(r + 1) * n, :].astype(o_ref.dtype)


def _ln_proj(x, ln_g, ln_b, w, bias, *, dil, tm, tn, name):
    bsz, seq, dm = x.shape
    n = w.shape[1]
    slab_rows = tm if dil > 1 else F32_ROWS
    return pl.pallas_call(
        functools.partial(_ln_proj_kernel, dil=dil, tm=tm),
        grid=(bsz, seq // tm, n // tn),
        in_specs=[
            pl.BlockSpec((None, tm, dm), lambda b, i, j: (b, i, 0)),
            pl.BlockSpec((1, dm), lambda b, i, j: (0, 0)),
            pl.BlockSpec((1, dm), lambda b, i, j: (0, 0)),
            pl.BlockSpec((dm, tn), lambda b, i, j: (0, j)),
            pl.BlockSpec((1, tn), lambda b, i, j: (0, j)),
        ],
        out_specs=pl.BlockSpec((None, dil, tm // dil, tn), lambda b, i, j: (b, 0, i, j)),
        out_shape=jax.ShapeDtypeStruct((bsz, dil, seq // dil, n), jnp.bfloat16),
        scratch_shapes=[pltpu.VMEM((tm, dm), jnp.bfloat16),
                        pltpu.VMEM((dm // LANES, slab_rows, LANES), jnp.float32)],
        compiler_params=pltpu.CompilerParams(
            dimension_semantics=("parallel", "parallel", "arbitrary"),
            vmem_limit_bytes=VMEM_LIMIT),
        name=name,
    )(x, ln_g, ln_b, w, bias)


def _attn_kernel(q_ref, kp_ref, kc_ref, kn_ref, vp_ref, vc_ref, vn_ref, o_ref, st_ref,
                 kbuf, vbuf, bias_ref, *, dil, slopes, sub_len, tq):
    i = pl.program_id(2)
    kbuf[0:RADIUS, :] = kp_ref[...]
    kbuf[RADIUS:RADIUS + tq, :] = kc_ref[...]
    kbuf[RADIUS + tq:, :] = kn_ref[...]
    vbuf[0:RADIUS, :] = vp_ref[...]
    vbuf[RADIUS:RADIUS + tq, :] = vc_ref[...]
    vbuf[RADIUS + tq:, :] = vn_ref[...]

    row = lax.broadcasted_iota(jnp.int32, (QBLK, KBLK), 0)
    col = lax.broadcasted_iota(jnp.int32, (QBLK, KBLK), 1)
    rel = jnp.abs(col - RADIUS - row)
    band = rel <= RADIUS
    dist = (rel * dil).astype(jnp.float32)
    for h in range(HEADS_PER_GROUP):
        bias_ref[h] = -slopes[h] * dist
    head_of_lane = lax.broadcasted_iota(jnp.int32, (1, DOT_W), 1) // HEAD_DIM
    f32 = jnp.float32

    def body(j, carry):
        r0 = pl.multiple_of(j * QBLK, QBLK)
        kpos = col + (i * tq + j * QBLK - RADIUS)
        valid = band & (kpos >= 0) & (kpos < sub_len)
        for part in range(HEADS_PER_GROUP // HEADS_PER_DOT):
            lanes = slice(part * DOT_W, (part + 1) * DOT_W)
            qb = q_ref[pl.ds(r0, QBLK), lanes]
            kb = kbuf[pl.ds(r0, KBLK), lanes]
            vb = vbuf[pl.ds(r0, KBLK), lanes]
            q_heads = jnp.concatenate(
                [jnp.where(head_of_lane == hh, qb, jnp.zeros_like(qb)) for hh in range(HEADS_PER_DOT)],
                axis=0)
            s_all = lax.dot_general(q_heads, kb, (((1,), (1,)), ((), ())), preferred_element_type=f32)
            p_heads, inv_den, lse = [], [], []
            for hh in range(HEADS_PER_DOT):
                s = s_all[hh * QBLK:(hh + 1) * QBLK, :] + bias_ref[part * HEADS_PER_DOT + hh]
                s = jnp.where(valid, s, MASK_VALUE)
                m = jnp.max(s, -1, keepdims=True)
                p = jnp.exp(s - m)
                den = jnp.sum(p, -1, keepdims=True)
                p_heads.append(p.astype(jnp.bfloat16))
                inv_den.append(1.0 / den)
                lse.append(m + jnp.log(den))
            o_all = jnp.dot(jnp.concatenate(p_heads, axis=0), vb, preferred_element_type=f32)
            o = jnp.zeros((QBLK, DOT_W), f32)
            st = jnp.zeros((QBLK, DOT_W), f32)
            for hh in range(HEADS_PER_DOT):
                sel = head_of_lane == hh
                o = jnp.where(sel, o_all[hh * QBLK:(hh + 1) * QBLK, :] * inv_den[hh], o)
                st = jnp.where(sel, lse[hh], st)
            o_ref[pl.ds(r0, QBLK), lanes] = o.astype(o_ref.dtype)
            st_ref[pl.ds(r0, QBLK), lanes] = st
        return carry

    lax.fori_loop(0, tq // QBLK, body, 0)


def _attention(qkv, *, dil, slopes, tq, name):
    bsz, _, sub, _ = qkv.shape
    w = ATT_GROUP_W
    nq = tq // RADIUS
    nblk = sub // RADIUS
    cur = lambda c: pl.BlockSpec((None, None, tq, w), lambda b, r, i: (b, r, i, c))
    prev = lambda c: pl.BlockSpec((None, None, RADIUS, w),
                                  lambda b, r, i: (b, r, jnp.maximum(i * nq - 1, 0), c))
    nxt = lambda c: pl.BlockSpec((None, None, RADIUS, w),
                                 lambda b, r, i: (b, r, jnp.minimum((i + 1) * nq, nblk - 1), c))
    out_spec = pl.BlockSpec((None, None, tq, w), lambda b, r, i: (b, r, i, 0))
    return pl.pallas_call(
        functools.partial(_attn_kernel, dil=dil, slopes=tuple(float(s) for s in slopes),
                          sub_len=sub, tq=tq),
        grid=(bsz, dil, sub // tq),
        in_specs=[cur(0), prev(1), cur(1), nxt(1), prev(2), cur(2), nxt(2)],
        out_specs=[out_spec, out_spec],
        out_shape=[jax.ShapeDtypeStruct((bsz, dil, sub, w), jnp.bfloat16),
                   jax.ShapeDtypeStruct((bsz, dil, sub, w), jnp.float32)],
        scratch_shapes=[pltpu.VMEM((tq + 2 * RADIUS, w), jnp.bfloat16),
                        pltpu.VMEM((tq + 2 * RADIUS, w), jnp.bfloat16),
                        pltpu.VMEM((HEADS_PER_GROUP, QBLK, KBLK), jnp.float32)],
        compiler_params=pltpu.CompilerParams(
            dimension_semantics=("parallel", "parallel", "parallel"),
            vmem_limit_bytes=VMEM_LIMIT),
        name=name,
    )(qkv, qkv, qkv, qkv, qkv, qkv, qkv)


def _dwconv3_rows(u, prev_row, next_row, w_ref):
    n = u.shape[0]
    row = lax.broadcasted_iota(jnp.int32, u.shape, 0)
    u_dn = jnp.where(row == 0, prev_row, pltpu.roll(u, 1, axis=0))
    u_up = jnp.where(row == n - 1, next_row, pltpu.roll(u, n - 1, axis=0))
    return u_dn * w_ref[0:1, :] + u * w_ref[1:2, :] + u_up * w_ref[2:3, :]


def _to_natural_order(src_ref, slab_ref):
    dil, n, w = src_ref.shape
    for r in range(dil):
        blk = src_ref[r].astype(jnp.float32)
        for c in range(w // LANES):
            slab_ref[c, pl.ds(r, n, stride=dil), :] = blk[:, c * LANES:(c + 1) * LANES]
    return jnp.concatenate([slab_ref[c] for c in range(w // LANES)], axis=-1)


def _mix_kernel(x_ref, gb_ref, gc_ref, hn_ref, ga_ref, gbb_ref,
                gcp_ref, gcn_ref, hnp_ref, hnn_ref,
                o0_ref, o1_ref, o2_ref, s0_ref, s1_ref, s2_ref,
                cw_ref, wa_ref, wb_ref, wo_ref, bo_ref,
                g0_ref, b0_ref, g1_ref, b1_ref, out_ref,
                slab_o1, slab_s1, slab_o2, slab_s2):
    i = pl.program_id(1)
    last = pl.num_programs(1) - 1
    f32 = jnp.float32
    h = _layer_norm(x_ref[...], g0_ref[...], b0_ref[...])

    u = gc_ref[...].astype(f32) * hn_ref[...].astype(f32)
    u_prev = (gcp_ref[...].astype(f32) * hnp_ref[...].astype(f32))[BF16_ROWS - 1:BF16_ROWS, :]
    u_next = (gcn_ref[...].astype(f32) * hnn_ref[...].astype(f32))[0:1, :]
    u_prev = jnp.where(i > 0, u_prev, 0.0)
    u_next = jnp.where(i < last, u_next, 0.0)
    conv = _dwconv3_rows(u, u_prev, u_next, cw_ref)
    y_a = jnp.dot((gb_ref[...].astype(f32) * conv).astype(jnp.bfloat16), wa_ref[...],
                  preferred_element_type=f32)

    o0, s0 = o0_ref[0].astype(f32), s0_ref[0]
    o1, s1 = _to_natural_order(o1_ref, slab_o1), _to_natural_order(s1_ref, slab_s1)
    o2, s2 = _to_natural_order(o2_ref, slab_o2), _to_natural_order(s2_ref, slab_s2)
    m = jnp.maximum(jnp.maximum(s0, s1), s2)
    e0, e1, e2 = jnp.exp(s0 - m), jnp.exp(s1 - m), jnp.exp(s2 - m)
    comb = (e0 * o0 + e1 * o1 + e2 * o2) / (e0 + e1 + e2)
    y_b = jnp.dot(comb.astype(jnp.bfloat16), wb_ref[...], preferred_element_type=f32)

    merged = (jax.nn.sigmoid(ga_ref[...].astype(f32)) * y_a
              + jax.nn.sigmoid(gbb_ref[...].astype(f32)) * y_b)
    mix = jnp.dot(merged.astype(jnp.bfloat16), wo_ref[...], preferred_element_type=f32) + bo_ref[...]
    out_ref[...] = _layer_norm(ALPHA * h + mix, g1_ref[...], b1_ref[...])


def _mix(x, pm, outs, stats, cw, wa, wb, wo, bo, g0, b0, g1, b1, *, tm):
    bsz, seq, dm = x.shape
    nhalo = tm // BF16_ROWS
    nrow_blk = seq // BF16_ROWS
    col = lambda c: pl.BlockSpec((None, tm, dm), lambda b, i: (b, i, c))
    prev = lambda c: pl.BlockSpec((None, BF16_ROWS, dm),
                                  lambda b, i: (b, jnp.maximum(i * nhalo - 1, 0), c))
    nxt = lambda c: pl.BlockSpec((None, BF16_ROWS, dm),
                                 lambda b, i: (b, jnp.minimum((i + 1) * nhalo, nrow_blk - 1), c))
    att = lambda d: pl.BlockSpec((None, d, tm // d, ATT_GROUP_W), lambda b, i: (b, 0, i, 0))
    att_specs = [att(d) for _, d in GROUPS]
    full = lambda a: pl.BlockSpec(a.shape, lambda b, i: (0,) * a.ndim)
    weights = (cw, wa, wb, wo, bo, g0, b0, g1, b1)
    slab = pltpu.VMEM((ATT_GROUP_W // LANES, tm, LANES), jnp.float32)
    return pl.pallas_call(
        _mix_kernel,
        grid=(bsz, seq // tm),
        in_specs=[col(0), col(0), col(1), col(2), col(3), col(4),
                  prev(1), nxt(1), prev(2), nxt(2)]
                 + att_specs + att_specs + [full(a) for a in weights],
        out_specs=pl.BlockSpec((None, tm, dm), lambda b, i: (b, i, 0)),
        out_shape=jax.ShapeDtypeStruct((bsz, seq, dm), jnp.float32),
        scratch_shapes=[slab, slab, slab, slab],
        compiler_params=pltpu.CompilerParams(
            dimension_semantics=("parallel", "parallel"),
            vmem_limit_bytes=VMEM_LIMIT),
        name="mix_merge",
    )(x, pm, pm, pm, pm, pm, pm, pm, pm, pm, *outs, *stats, *weights)


def _ffn_kernel(h_ref, hp_ref, hn_ref, wua_ref, wug_ref, bua_ref, bug_ref, cw_ref, cb_ref,
                wd_ref, bd_ref, g_ref, b_ref, out_ref, lhs_ref, acc_ref, *, tm):
    i = pl.program_id(1)
    c = pl.program_id(2)
    last = pl.num_programs(1) - 1
    f32 = jnp.float32

    @pl.when(c == 0)
    def _():
        lhs_ref[0:tm, :] = h_ref[...].astype(jnp.bfloat16)
        halo = jnp.concatenate([hp_ref[...], hn_ref[...]], axis=0)
        lhs_ref[tm:, :] = halo.astype(jnp.bfloat16)
        acc_ref[...] = jnp.zeros_like(acc_ref)

    a_all = jnp.dot(lhs_ref[...], wua_ref[...], preferred_element_type=f32) + bua_ref[...]
    gate = jnp.dot(lhs_ref[0:tm, :], wug_ref[...], preferred_element_type=f32) + bug_ref[...]
    a = a_all[0:tm, :]
    a_prev = jnp.where(i > 0, a_all[tm + F32_ROWS - 1:tm + F32_ROWS, :], 0.0)
    a_next = jnp.where(i < last, a_all[tm + F32_ROWS:tm + F32_ROWS + 1, :], 0.0)
    pre = _dwconv3_rows(a, a_prev, a_next, cw_ref) + cb_ref[...]
    f = 0.5 * pre * (1.0 + lax.erf(pre * (2.0 ** -0.5))) * gate
    acc_ref[...] += jnp.dot(f.astype(jnp.bfloat16), wd_ref[...], preferred_element_type=f32)

    @pl.when(c == pl.num_programs(2) - 1)
    def _():
        out_ref[...] = _layer_norm(ALPHA * h_ref[...] + acc_ref[...] + bd_ref[...],
                                   g_ref[...], b_ref[...])


def _ffn(h1, wu, bu, cw, cb, wd, bd, g, b, *, tm, ck):
    bsz, seq, dm = h1.shape
    nch = D_FF // ck
    nhalo = tm // F32_ROWS
    nrow_blk = seq // F32_ROWS
    const = lambda a: pl.BlockSpec(a.shape, lambda bb, i, c: (0,) * a.ndim)
    return pl.pallas_call(
        functools.partial(_ffn_kernel, tm=tm),
        grid=(bsz, seq // tm, nch),
        in_specs=[
            pl.BlockSpec((None, tm, dm), lambda bb, i, c: (bb, i, 0)),
            pl.BlockSpec((None, F32_ROWS, dm), lambda bb, i, c: (bb, jnp.maximum(i * nhalo - 1, 0), 0)),
            pl.BlockSpec((None, F32_ROWS, dm),
                         lambda bb, i, c: (bb, jnp.minimum((i + 1) * nhalo, nrow_blk - 1), 0)),
            pl.BlockSpec((dm, ck), lambda bb, i, c: (0, c)),
            pl.BlockSpec((dm, ck), lambda bb, i, c: (0, nch + c)),
            pl.BlockSpec((1, ck), lambda bb, i, c: (0, c)),
            pl.BlockSpec((1, ck), lambda bb, i, c: (0, nch + c)),
            pl.BlockSpec((3, ck), lambda bb, i, c: (0, c)),
            pl.BlockSpec((1, ck), lambda bb, i, c: (0, c)),
            pl.BlockSpec((ck, dm), lambda bb, i, c: (c, 0)),
            const(bd), const(g), const(b),
        ],
        out_specs=pl.BlockSpec((None, tm, dm), lambda bb, i, c: (bb, i, 0)),
        out_shape=jax.ShapeDtypeStruct((bsz, seq, dm), jnp.float32),
        scratch_shapes=[pltpu.VMEM((tm + 2 * F32_ROWS, dm), jnp.bfloat16),
                        pltpu.VMEM((tm, dm), jnp.float32)],
        compiler_params=pltpu.CompilerParams(
            dimension_semantics=("parallel", "parallel", "arbitrary"),
            vmem_limit_bytes=VMEM_LIMIT),
        name="ffn",
    )(h1, h1, h1, wu, wu, bu, bu, cw, cb, wd, bd, g, b)


def kernel(x, ln0_g, ln0_b, w_in, b_in, conv_w, w_a, w_b, w_o, b_o, ln1_g, ln1_b,
           w_up, b_up, ffn_conv_w, ffn_conv_b, w_down, b_down, ln2_g, ln2_b):
    bf16 = jnp.bfloat16
    row = lambda v: v.reshape(1, -1)
    slopes = _alibi_slopes()
    g0, b0 = row(ln0_g), row(ln0_b)
    bsz, seq, _ = x.shape
    l = 0
    wi, bi = w_in[l], b_in[l]
    w_main = jnp.concatenate([wi[:, OFF_B:OFF_Q], wi[:, OFF_GA:]], axis=1).astype(bf16)
    b_main = jnp.concatenate([bi[OFF_B:OFF_Q], bi[OFF_GA:]])
    pm = _ln_proj(x, g0, b0, w_main, row(b_main), dil=1, tm=1024, tn=1024,
                  name="proj_main").reshape(bsz, seq, -1)
    outs, stats = [], []
    for g, (_, dil) in enumerate(GROUPS):
        lo, hi = g * ATT_GROUP_W, (g + 1) * ATT_GROUP_W
        scale = HEAD_DIM ** -0.5
        w_qkv = jnp.concatenate([wi[:, OFF_Q + lo:OFF_Q + hi] * scale, wi[:, OFF_K + lo:OFF_K + hi],
                                 wi[:, OFF_V + lo:OFF_V + hi]], axis=1).astype(bf16)
        b_qkv = jnp.concatenate([bi[OFF_Q + lo:OFF_Q + hi] * scale, bi[OFF_K + lo:OFF_K + hi],
                                 bi[OFF_V + lo:OFF_V + hi]])
        qkv = _ln_proj(x, g0, b0, w_qkv, row(b_qkv), dil=dil, tm=1024, tn=3 * ATT_GROUP_W,
                       name=f"proj_qkv{g}")
        o, st = _attention(qkv, dil=dil, slopes=slopes[g], tq=min(1024, seq // dil), name=f"attn{g}")
        outs.append(o)
        stats.append(st)
    h1 = _mix(x, pm, outs, stats, conv_w[l], w_a[l].astype(bf16), w_b[l].astype(bf16),
              w_o[l].astype(bf16), row(b_o[l]), g0, b0, row(ln1_g[l]), row(ln1_b[l]), tm=512)
    return _ffn(h1, w_up[l].astype(bf16), row(b_up[l]), ffn_conv_w[l], row(ffn_conv_b[l]),
                w_down[l].astype(bf16), row(b_down[l]), row(ln2_g[l]), row(ln2_b[l]),
                tm=512, ck=D_FF // 2)
```

```python
import functools

import numpy as np
import jax
import jax.numpy as jnp
from jax import lax
from jax.experimental import pallas as pl
from jax.experimental.pallas import tpu as pltpu

D_MODEL = 1024
D_CONV = D_MODEL
HEAD_DIM = 64
HEADS_PER_GROUP = 8
GROUPS = ((128, 1), (512, 4), (2048, 16))
N_GROUPS = len(GROUPS)
N_ATT_HEADS = N_GROUPS * HEADS_PER_GROUP
ATT_GROUP_W = HEADS_PER_GROUP * HEAD_DIM
ATT_QKV_W = N_GROUPS * ATT_GROUP_W
D_FF = 2816
LN_EPS = 1e-5
DEPTH = 1
assert DEPTH == 1
ALPHA = (2.0 * DEPTH) ** 0.25
MASK_VALUE = -1e30

OFF_B = 0
OFF_C = OFF_B + D_CONV
OFF_H = OFF_C + D_CONV
OFF_Q = OFF_H + D_CONV
OFF_K = OFF_Q + ATT_QKV_W
OFF_V = OFF_K + ATT_QKV_W
OFF_GA = OFF_V + ATT_QKV_W
OFF_GB = OFF_GA + D_MODEL

RADIUS = 64
QBLK = 2 * RADIUS
KBLK = QBLK + 2 * RADIUS
LANES = 128
BF16_ROWS = 16
F32_ROWS = 8
HEADS_PER_DOT = 4
DOT_W = HEADS_PER_DOT * HEAD_DIM
VMEM_LIMIT = 56 * 1024 * 1024

assert all(w // (2 * d) == RADIUS for w, d in GROUPS)


def _layer_norm(xf, g, b):
    mu = jnp.mean(xf, -1, keepdims=True)
    xc = xf - mu
    var = jnp.mean(xc * xc, -1, keepdims=True)
    return xc * lax.rsqrt(var + LN_EPS) * g + b


def _alibi_slopes():
    n = N_ATT_HEADS
    return np.exp2(-8.0 * np.arange(1, n + 1, dtype=np.float64) / n).reshape(N_GROUPS, HEADS_PER_GROUP)


def _resident(a):
    return pl.BlockSpec(a.shape, lambda b, i: (0,) * a.ndim, pipeline_mode=pl.Buffered(1))


def _qkv_proj_kernel(x_ref, g_ref, b_ref, w_ref, bias_ref, *refs, tm):
    o_refs, (hb_ref, slab_ref) = refs[:N_GROUPS], refs[N_GROUPS:]
    h = _layer_norm(x_ref[...], g_ref[...], b_ref[...])
    n_slabs = D_MODEL // LANES
    for c in range(n_slabs):
        slab_ref[c] = h[:, c * LANES:(c + 1) * LANES]
    for g, (_, dil) in enumerate(GROUPS):
        n = tm // dil
        if dil == 1:
            hb_ref[g] = h.astype(jnp.bfloat16)
        else:
            for r in range(dil):
                for c in range(n_slabs):
                    hb_ref[g, r * n:(r + 1) * n, c * LANES:(c + 1) * LANES] = (
                        slab_ref[c, pl.ds(r, n, stride=dil), :].astype(jnp.bfloat16))
        acc = jnp.dot(hb_ref[g], w_ref[g], preferred_element_type=jnp.float32) + bias_ref[g]
        for r in range(dil):
            o_refs[g][r] = acc[r * n:(r + 1) * n, :].astype(o_refs[g].dtype)


def _qkv_proj(x, ln_g, ln_b, w, bias, *, tm):
    bsz, seq, dm = x.shape
    n = w.shape[-1]
    return pl.pallas_call(
        functools.partial(_qkv_proj_kernel, tm=tm),
        grid=(bsz, seq // tm),
        in_specs=[pl.BlockSpec((None, tm, dm), lambda b, i: (b, i, 0)),
                  _resident(ln_g), _resident(ln_b), _resident(w), _resident(bias)],
        out_specs=[pl.BlockSpec((None, d, tm // d, n), lambda b, i: (b, 0, i, 0)) for _, d in GROUPS],
        out_shape=[jax.ShapeDtypeStruct((bsz, d, seq // d, n), jnp.bfloat16) for _, d in GROUPS],
        scratch_shapes=[pltpu.VMEM((N_GROUPS, tm, dm), jnp.bfloat16),
                        pltpu.VMEM((dm // LANES, tm, LANES), jnp.float32)],
        compiler_params=pltpu.CompilerParams(
            dimension_semantics=("parallel", "parallel"),
            vmem_limit_bytes=VMEM_LIMIT),
        name="proj_qkv",
    )(x, ln_g, ln_b, w, bias)


def _attn_kernel(q_ref, kp_ref, kc_ref, kn_ref, vp_ref, vc_ref, vn_ref, o_ref, st_ref,
                 kbuf, vbuf, bias_ref, *, dil, slopes, sub_len, tq):
    i = pl.program_id(2)
    kbuf[0:RADIUS, :] = kp_ref[...]
    kbuf[RADIUS:RADIUS + tq, :] = kc_ref[...]
    kbuf[RADIUS + tq:, :] = kn_ref[...]
    vbuf[0:RADIUS, :] = vp_ref[...]
    vbuf[RADIUS:RADIUS + tq, :] = vc_ref[...]
    vbuf[RADIUS + tq:, :] = vn_ref[...]

    row = lax.broadcasted_iota(jnp.int32, (QBLK, KBLK), 0)
    col = lax.broadcasted_iota(jnp.int32, (QBLK, KBLK), 1)
    rel = jnp.abs(col - RADIUS - row)
    band = rel <= RADIUS
    dist = (rel * dil).astype(jnp.float32)
    for h in range(HEADS_PER_GROUP):
        bias_ref[h] = -slopes[h] * dist
    head_of_lane = lax.broadcasted_iota(jnp.int32, (1, DOT_W), 1) // HEAD_DIM
    f32 = jnp.float32

    def body(j, carry):
        r0 = pl.multiple_of(j * QBLK, QBLK)
        kpos = col + (i * tq + j * QBLK - RADIUS)
        valid = band & (kpos >= 0) & (kpos < sub_len)
        for part in range(HEADS_PER_GROUP // HEADS_PER_DOT):
            lanes = slice(part * DOT_W, (part + 1) * DOT_W)
            qb = q_ref[pl.ds(r0, QBLK), lanes]
            kb = kbuf[pl.ds(r0, KBLK), lanes]
            vb = vbuf[pl.ds(r0, KBLK), lanes]
            q_heads = jnp.concatenate(
                [jnp.where(head_of_lane == hh, qb, jnp.zeros_like(qb)) for hh in range(HEADS_PER_DOT)],
                axis=0)
            s_all = lax.dot_general(q_heads, kb, (((1,), (1,)), ((), ())), preferred_element_type=f32)
            p_heads, inv_den, lse = [], [], []
            for hh in range(HEADS_PER_DOT):
                s = s_all[hh * QBLK:(hh + 1) * QBLK, :] + bias_ref[part * HEADS_PER_DOT + hh]
                s = jnp.where(valid, s, MASK_VALUE)
                m = jnp.max(s, -1, keepdims=True)
                p = jnp.exp(s - m)
                den = jnp.sum(p, -1, keepdims=True)
                p_heads.append(p.astype(jnp.bfloat16))
                inv_den.append(1.0 / den)
                lse.append(m + jnp.log(den))
            o_all = jnp.dot(jnp.concatenate(p_heads, axis=0), vb, preferred_element_type=f32)
            o = jnp.zeros((QBLK, DOT_W), f32)
            st = jnp.zeros((QBLK, DOT_W), f32)
            for hh in range(HEADS_PER_DOT):
                sel = head_of_lane == hh
                o = jnp.where(sel, o_all[hh * QBLK:(hh + 1) * QBLK, :] * inv_den[hh], o)
                st = jnp.where(sel, lse[hh], st)
            o_ref[pl.ds(r0, QBLK), lanes] = o.astype(o_ref.dtype)
            st_ref[pl.ds(r0, QBLK), lanes] = st
        return carry

    lax.fori_loop(0, tq // QBLK, body, 0)


def _attention(qkv, *, dil, slopes, tq, name):
    bsz, _, sub, _ = qkv.shape
    w = ATT_GROUP_W
    nq = tq // RADIUS
    nblk = sub // RADIUS
    cur = lambda c: pl.BlockSpec((None, None, tq, w), lambda b, r, i: (b, r, i, c))
    prev = lambda c: pl.BlockSpec((None, None, RADIUS, w),
                                  lambda b, r, i: (b, r, jnp.maximum(i * nq - 1, 0), c))
    nxt = lambda c: pl.BlockSpec((None, None, RADIUS, w),
                                 lambda b, r, i: (b, r, jnp.minimum((i + 1) * nq, nblk - 1), c))
    out_spec = pl.BlockSpec((None, None, tq, w), lambda b, r, i: (b, r, i, 0))
    return pl.pallas_call(
        functools.partial(_attn_kernel, dil=dil, slopes=tuple(float(s) for s in slopes),
                          sub_len=sub, tq=tq),
        grid=(bsz, dil, sub // tq),
        in_specs=[cur(0), prev(1), cur(1), nxt(1), prev(2), cur(2), nxt(2)],
        out_specs=[out_spec, out_spec],
        out_shape=[jax.ShapeDtypeStruct((bsz, dil, sub, w), jnp.bfloat16),
                   jax.ShapeDtypeStruct((bsz, dil, sub, w), jnp.float32)],
        scratch_shapes=[pltpu.VMEM((tq + 2 * RADIUS, w), jnp.bfloat16),
                        pltpu.VMEM((tq + 2 * RADIUS, w), jnp.bfloat16),
                        pltpu.VMEM((HEADS_PER_GROUP, QBLK, KBLK), jnp.float32)],
        compiler_params=pltpu.CompilerParams(
            dimension_semantics=("parallel", "parallel", "parallel"),
            vmem_limit_bytes=VMEM_LIMIT),
        name=name,
    )(qkv, qkv, qkv, qkv, qkv, qkv, qkv)


def _dwconv3_rows(u, prev_row, next_row, w_ref):
    n = u.shape[0]
    row = lax.broadcasted_iota(jnp.int32, u.shape, 0)
    u_dn = jnp.where(row == 0, prev_row, pltpu.roll(u, 1, axis=0))
    u_up = jnp.where(row == n - 1, next_row, pltpu.roll(u, n - 1, axis=0))
    return u_dn * w_ref[0:1, :] + u * w_ref[1:2, :] + u_up * w_ref[2:3, :]


def _to_natural_order(src_ref, slab_ref):
    dil, n, w = src_ref.shape
    for r in range(dil):
        blk = src_ref[r].astype(jnp.float32)
        for c in range(w // LANES):
            slab_ref[c, pl.ds(r, n, stride=dil), :] = blk[:, c * LANES:(c + 1) * LANES]
    return jnp.concatenate([slab_ref[c] for c in range(w // LANES)], axis=-1)


def _mix_kernel(x_ref, xp_ref, xn_ref,
                o0_ref, o1_ref, o2_ref, s0_ref, s1_ref, s2_ref,
                wgb_ref, bgb_ref, wch_ref, bch_ref, wg_ref, bg_ref,
                cw_ref, wa_ref, wb_ref, wo_ref, bo_ref,
                g0_ref, b0_ref, g1_ref, b1_ref, out_ref,
                slab_o1, slab_s1, slab_o2, slab_s2, *, tm):
    i = pl.program_id(1)
    last = pl.num_programs(1) - 1
    f32 = jnp.float32
    x_all = jnp.concatenate([x_ref[...], xp_ref[...], xn_ref[...]], axis=0)
    h_all = _layer_norm(x_all, g0_ref[...], b0_ref[...])
    h = h_all[0:tm, :]
    lhs_all = h_all.astype(jnp.bfloat16)
    lhs = lhs_all[0:tm, :]

    ch = jnp.dot(lhs_all, wch_ref[...], preferred_element_type=f32) + bch_ref[...]
    u_all = ch[:, 0:D_CONV] * ch[:, D_CONV:]
    u = u_all[0:tm, :]
    u_prev = jnp.where(i > 0, u_all[tm + F32_ROWS - 1:tm + F32_ROWS, :], 0.0)
    u_next = jnp.where(i < last, u_all[tm + F32_ROWS:tm + F32_ROWS + 1, :], 0.0)
    conv = _dwconv3_rows(u, u_prev, u_next, cw_ref)
    gate_b = jnp.dot(lhs, wgb_ref[...], preferred_element_type=f32) + bgb_ref[...]
    y_a = jnp.dot((gate_b * conv).astype(jnp.bfloat16), wa_ref[...], preferred_element_type=f32)

    o0, s0 = o0_ref[0].astype(f32), s0_ref[0]
    o1, s1 = _to_natural_order(o1_ref, slab_o1), _to_natural_order(s1_ref, slab_s1)
    o2, s2 = _to_natural_order(o2_ref, slab_o2), _to_natural_order(s2_ref, slab_s2)
    m = jnp.maximum(jnp.maximum(s0, s1), s2)
    e0, e1, e2 = jnp.exp(s0 - m), jnp.exp(s1 - m), jnp.exp(s2 - m)
    comb = (e0 * o0 + e1 * o1 + e2 * o2) / (e0 + e1 + e2)
    y_b = jnp.dot(comb.astype(jnp.bfloat16), wb_ref[...], preferred_element_type=f32)

    gates = jnp.dot(lhs, wg_ref[...], preferred_element_type=f32) + bg_ref[...]
    merged = (jax.nn.sigmoid(gates[:, 0:D_MODEL]) * y_a
              + jax.nn.sigmoid(gates[:, D_MODEL:]) * y_b)
    mix = jnp.dot(merged.astype(jnp.bfloat16), wo_ref[...], preferred_element_type=f32) + bo_ref[...]
    out_ref[...] = _layer_norm(ALPHA * h + mix, g1_ref[...], b1_ref[...])


def _mix(x, outs, stats, weights, *, tm):
    bsz, seq, dm = x.shape
    nhalo = tm // F32_ROWS
    nrow_blk = seq // F32_ROWS
    att = lambda d: pl.BlockSpec((None, d, tm // d, ATT_GROUP_W), lambda b, i: (b, 0, i, 0))
    att_specs = [att(d) for _, d in GROUPS]
    slab = pltpu.VMEM((ATT_GROUP_W // LANES, tm, LANES), jnp.float32)
    return pl.pallas_call(
        functools.partial(_mix_kernel, tm=tm),
        grid=(bsz, seq // tm),
        in_specs=[pl.BlockSpec((None, tm, dm), lambda b, i: (b, i, 0)),
                  pl.BlockSpec((None, F32_ROWS, dm), lambda b, i: (b, jnp.maximum(i * nhalo - 1, 0), 0)),
                  pl.BlockSpec((None, F32_ROWS, dm),
                               lambda b, i: (b, jnp.minimum((i + 1) * nhalo, nrow_blk - 1), 0))]
                 + att_specs + att_specs + [_resident(a) for a in weights],
        out_specs=pl.BlockSpec((None, tm, dm), lambda b, i: (b, i, 0)),
        out_shape=jax.ShapeDtypeStruct((bsz, seq, dm), jnp.float32),
        scratch_shapes=[slab, slab, slab, slab],
        compiler_params=pltpu.CompilerParams(
            dimension_semantics=("parallel", "parallel"),
            vmem_limit_bytes=VMEM_LIMIT),
        name="mix",
    )(x, x, x, *outs, *stats, *weights)


def _ffn_kernel(h_ref, hp_ref, hn_ref, wua_ref, wug_ref, bua_ref, bug_ref, cw_ref, cb_ref,
                wd_ref, bd_ref, g_ref, b_ref, out_ref, lhs_ref, acc_ref, *, tm):
    i = pl.program_id(1)
    c = pl.program_id(2)
    last = pl.num_programs(1) - 1
    f32 = jnp.float32

    @pl.when(c == 0)
    def _():
        lhs_ref[0:tm, :] = h_ref[...].astype(jnp.bfloat16)
        halo = jnp.concatenate([hp_ref[...], hn_ref[...]], axis=0)
        lhs_ref[tm:, :] = halo.astype(jnp.bfloat16)
        acc_ref[...] = jnp.zeros_like(acc_ref)

    a_all = jnp.dot(lhs_ref[...], wua_ref[...], preferred_element_type=f32) + bua_ref[...]
    gate = jnp.dot(lhs_ref[0:tm, :], wug_ref[...], preferred_element_type=f32) + bug_ref[...]
    a = a_all[0:tm, :]
    a_prev = jnp.where(i > 0, a_all[tm + F32_ROWS - 1:tm + F32_ROWS, :], 0.0)
    a_next = jnp.where(i < last, a_all[tm + F32_ROWS:tm + F32_ROWS + 1, :], 0.0)
    pre = _dwconv3_rows(a, a_prev, a_next, cw_ref) + cb_ref[...]
    f = 0.5 * pre * (1.0 + lax.erf(pre * (2.0 ** -0.5))) * gate
    acc_ref[...] += jnp.dot(f.astype(jnp.bfloat16), wd_ref[...], preferred_element_type=f32)

    @pl.when(c == pl.num_programs(2) - 1)
    def _():
        out_ref[...] = _layer_norm(ALPHA * h_ref[...] + acc_ref[...] + bd_ref[...],
                                   g_ref[...], b_ref[...])


def _ffn(h1, wu, bu, cw, cb, wd, bd, g, b, *, tm, ck):
    bsz, seq, dm = h1.shape
    nch = D_FF // ck
    nhalo = tm // F32_ROWS
    nrow_blk = seq // F32_ROWS
    const = lambda a: pl.BlockSpec(a.shape, lambda bb, i, c: (0,) * a.ndim)
    return pl.pallas_call(
        functools.partial(_ffn_kernel, tm=tm),
        grid=(bsz, seq // tm, nch),
        in_specs=[
            pl.BlockSpec((None, tm, dm), lambda bb, i, c: (bb, i, 0)),
            pl.BlockSpec((None, F32_ROWS, dm), lambda bb, i, c: (bb, jnp.maximum(i * nhalo - 1, 0), 0)),
            pl.BlockSpec((None, F32_ROWS, dm),
                         lambda bb, i, c: (bb, jnp.minimum((i + 1) * nhalo, nrow_blk - 1), 0)),
            pl.BlockSpec((dm, ck), lambda bb, i, c: (0, c)),
            pl.BlockSpec((dm, ck), lambda bb, i, c: (0, nch + c)),
            pl.BlockSpec((1, ck), lambda bb, i, c: (0, c)),
            pl.BlockSpec((1, ck), lambda bb, i, c: (0, nch + c)),
            pl.BlockSpec((3, ck), lambda bb, i, c: (0, c)),
            pl.BlockSpec((1, ck), lambda bb, i, c: (0, c)),
            pl.BlockSpec((ck, dm), lambda bb, i, c: (c, 0)),
            const(bd), const(g), const(b),
        ],
        out_specs=pl.BlockSpec((None, tm, dm), lambda bb, i, c: (bb, i, 0)),
        out_shape=jax.ShapeDtypeStruct((bsz, seq, dm), jnp.float32),
        scratch_shapes=[pltpu.VMEM((tm + 2 * F32_ROWS, dm), jnp.bfloat16),
                        pltpu.VMEM((tm, dm), jnp.float32)],
        compiler_params=pltpu.CompilerParams(
            dimension_semantics=("parallel", "parallel", "arbitrary"),
            vmem_limit_bytes=VMEM_LIMIT),
        name="ffn",
    )(h1, h1, h1, wu, wu, bu, bu, cw, cb, wd, bd, g, b)


def kernel(x, ln0_g, ln0_b, w_in, b_in, conv_w, w_a, w_b, w_o, b_o, ln1_g, ln1_b,
           w_up, b_up, ffn_conv_w, ffn_conv_b, w_down, b_down, ln2_g, ln2_b):
    bf16 = jnp.bfloat16
    row = lambda v: v.reshape(1, -1)
    slopes = _alibi_slopes()
    g0, b0 = row(ln0_g), row(ln0_b)
    bsz, seq, _ = x.shape
    l = 0
    wi, bi = w_in[l], b_in[l]
    scale = HEAD_DIM ** -0.5
    w_qkv, b_qkv = [], []
    for g in range(N_GROUPS):
        lo, hi = g * ATT_GROUP_W, (g + 1) * ATT_GROUP_W
        w_qkv.append(jnp.concatenate([wi[:, OFF_Q + lo:OFF_Q + hi] * scale, wi[:, OFF_K + lo:OFF_K + hi],
                                      wi[:, OFF_V + lo:OFF_V + hi]], axis=1).astype(bf16))
        b_qkv.append(jnp.concatenate([bi[OFF_Q + lo:OFF_Q + hi] * scale, bi[OFF_K + lo:OFF_K + hi],
                                      bi[OFF_V + lo:OFF_V + hi]]).reshape(1, -1))
    qkvs = _qkv_proj(x, g0, b0, jnp.stack(w_qkv), jnp.stack(b_qkv), tm=512)
    outs, stats = [], []
    for g, (_, dil) in enumerate(GROUPS):
        o, st = _attention(qkvs[g], dil=dil, slopes=slopes[g], tq=min(1024, seq // dil), name=f"attn{g}")
        outs.append(o)
        stats.append(st)
    mix_weights = (
        wi[:, OFF_B:OFF_C].astype(bf16), row(bi[OFF_B:OFF_C]),
        wi[:, OFF_C:OFF_Q].astype(bf16), row(bi[OFF_C:OFF_Q]),
        wi[:, OFF_GA:].astype(bf16), row(bi[OFF_GA:]),
        conv_w[l], w_a[l].astype(bf16), w_b[l].astype(bf16), w_o[l].astype(bf16), row(b_o[l]),
        g0, b0, row(ln1_g[l]), row(ln1_b[l]))
    h1 = _mix(x, outs, stats, mix_weights, tm=512)
    return _ffn(h1, w_up[l].astype(bf16), row(b_up[l]), ffn_conv_w[l], row(ffn_conv_b[l]),
                w_down[l].astype(bf16), row(b_down[l]), row(ln2_g[l]), row(ln2_b[l]),
                tm=512, ck=D_FF // 2)
```

```python
import functools

import numpy as np
import jax
import jax.numpy as jnp
from jax import lax
from jax.experimental import pallas as pl
from jax.experimental.pallas import tpu as pltpu

D_MODEL = 1024
D_CONV = D_MODEL
HEAD_DIM = 64
HEADS_PER_GROUP = 8
GROUPS = ((128, 1), (512, 4), (2048, 16))
N_GROUPS = len(GROUPS)
N_ATT_HEADS = N_GROUPS * HEADS_PER_GROUP
ATT_GROUP_W = HEADS_PER_GROUP * HEAD_DIM
ATT_QKV_W = N_GROUPS * ATT_GROUP_W
D_FF = 2816
LN_EPS = 1e-5
DEPTH = 1
assert DEPTH == 1
ALPHA = (2.0 * DEPTH) ** 0.25

OFF_B = 0
OFF_C = OFF_B + D_CONV
OFF_H = OFF_C + D_CONV
OFF_Q = OFF_H + D_CONV
OFF_K = OFF_Q + ATT_QKV_W
OFF_V = OFF_K + ATT_QKV_W
OFF_GA = OFF_V + ATT_QKV_W
OFF_GB = OFF_GA + D_MODEL

RADIUS = 64
QBLK = 2 * RADIUS
KBLK = QBLK + 2 * RADIUS
LANES = 128
BF16_ROWS = 16
F32_ROWS = 8
HEADS_PER_DOT = 4
DOT_W = HEADS_PER_DOT * HEAD_DIM
MXU_TILE = 256
VMEM_LIMIT = 56 * 1024 * 1024

assert all(w // (2 * d) == RADIUS for w, d in GROUPS)
assert D_FF % MXU_TILE == 0


def _chunks(total, width):
    full = [width] * (total // width)
    rem = total - sum(full)
    return tuple(full + ([rem] if rem else []))


def _layer_norm(xf, g, b):
    mu = jnp.mean(xf, -1, keepdims=True)
    xc = xf - mu
    var = jnp.mean(xc * xc, -1, keepdims=True)
    return xc * lax.rsqrt(var + LN_EPS) * g + b


def _alibi_slopes():
    n = N_ATT_HEADS
    return np.exp2(-8.0 * np.arange(1, n + 1, dtype=np.float64) / n).reshape(N_GROUPS, HEADS_PER_GROUP)


def _resident(a):
    return pl.BlockSpec(a.shape, lambda b, i: (0,) * a.ndim, pipeline_mode=pl.Buffered(1))


def _qkv_proj_kernel(x_ref, g_ref, b_ref, w_ref, bias_ref, *refs, tm):
    o_refs, (hb_ref, slab_ref) = refs[:N_GROUPS], refs[N_GROUPS:]
    h = _layer_norm(x_ref[...], g_ref[...], b_ref[...])
    n_slabs = D_MODEL // LANES
    for c in range(n_slabs):
        slab_ref[c] = h[:, c * LANES:(c + 1) * LANES]
    for g, (_, dil) in enumerate(GROUPS):
        n = tm // dil
        if dil == 1:
            hb_ref[g] = h.astype(jnp.bfloat16)
        else:
            for r in range(dil):
                for c in range(n_slabs):
                    hb_ref[g, r * n:(r + 1) * n, c * LANES:(c + 1) * LANES] = (
                        slab_ref[c, pl.ds(r, n, stride=dil), :].astype(jnp.bfloat16))
        acc = jnp.dot(hb_ref[g], w_ref[g], preferred_element_type=jnp.float32) + bias_ref[g]
        for r in range(dil):
            o_refs[g][r] = acc[r * n:(r + 1) * n, :].astype(o_refs[g].dtype)


def _qkv_proj(x, ln_g, ln_b, w, bias, *, tm):
    bsz, seq, dm = x.shape
    n = w.shape[-1]
    return pl.pallas_call(
        functools.partial(_qkv_proj_kernel, tm=tm),
        grid=(bsz, seq // tm),
        in_specs=[pl.BlockSpec((None, tm, dm), lambda b, i: (b, i, 0)),
                  _resident(ln_g), _resident(ln_b), _resident(w), _resident(bias)],
        out_specs=[pl.BlockSpec((None, d, tm // d, n), lambda b, i: (b, 0, i, 0)) for _, d in GROUPS],
        out_shape=[jax.ShapeDtypeStruct((bsz, d, seq // d, n), jnp.bfloat16) for _, d in GROUPS],
        scratch_shapes=[pltpu.VMEM((N_GROUPS, tm, dm), jnp.bfloat16),
                        pltpu.VMEM((dm // LANES, tm, LANES), jnp.float32)],
        compiler_params=pltpu.CompilerParams(
            dimension_semantics=("parallel", "parallel"),
            vmem_limit_bytes=VMEM_LIMIT),
        name="proj_qkv",
    )(x, ln_g, ln_b, w, bias)


def _attn_kernel(q_ref, kp_ref, kc_ref, kn_ref, vp_ref, vc_ref, vn_ref, o_ref, st_ref,
                 kbuf, vbuf, bias_ref, *, dil, slopes, sub_len, tq):
    i = pl.program_id(2)
    kbuf[0:RADIUS, :] = kp_ref[...]
    kbuf[RADIUS:RADIUS + tq, :] = kc_ref[...]
    kbuf[RADIUS + tq:, :] = kn_ref[...]
    vbuf[0:RADIUS, :] = vp_ref[...]
    vbuf[RADIUS:RADIUS + tq, :] = vc_ref[...]
    vbuf[RADIUS + tq:, :] = vn_ref[...]

    @pl.when((pl.program_id(0) == 0) & (pl.program_id(1) == 0) & (i == 0))
    def _():
        row = lax.broadcasted_iota(jnp.int32, (QBLK, KBLK), 0)
        col = lax.broadcasted_iota(jnp.int32, (QBLK, KBLK), 1)
        rel = jnp.abs(col - RADIUS - row)
        band = rel <= RADIUS
        dist = (rel * dil).astype(jnp.float32)
        for var, ok in enumerate((band, band & (col >= RADIUS), band & (col < KBLK - RADIUS))):
            for h in range(HEADS_PER_GROUP):
                bias_ref[var, h] = jnp.where(ok, -slopes[h] * dist, -jnp.inf)

    head_of_lane = lax.broadcasted_iota(jnp.int32, (1, DOT_W), 1) // HEAD_DIM
    f32 = jnp.float32
    last_blk = sub_len // QBLK - 1

    def body(j, carry):
        r0 = pl.multiple_of(j * QBLK, QBLK)
        blk = i * (tq // QBLK) + j
        var = jnp.where(blk == 0, 1, jnp.where(blk == last_blk, 2, 0))
        for part in range(HEADS_PER_GROUP // HEADS_PER_DOT):
            lanes = slice(part * DOT_W, (part + 1) * DOT_W)
            qb = q_ref[pl.ds(r0, QBLK), lanes]
            kb = kbuf[pl.ds(r0, KBLK), lanes]
            vb = vbuf[pl.ds(r0, KBLK), lanes]
            q_heads = jnp.concatenate(
                [jnp.where(head_of_lane == hh, qb, jnp.zeros_like(qb)) for hh in range(HEADS_PER_DOT)],
                axis=0)
            s_all = lax.dot_general(q_heads, kb, (((1,), (1,)), ((), ())), preferred_element_type=f32)
            p_heads, inv_den, lse = [], [], []
            for hh in range(HEADS_PER_DOT):
                s = s_all[hh * QBLK:(hh + 1) * QBLK, :] + bias_ref[var, part * HEADS_PER_DOT + hh]
                m = jnp.max(s, -1, keepdims=True)
                p = jnp.exp(s - m)
                den = jnp.sum(p, -1, keepdims=True)
                p_heads.append(p.astype(jnp.bfloat16))
                inv_den.append(1.0 / den)
                lse.append(m + jnp.log(den))
            o_all = jnp.dot(jnp.concatenate(p_heads, axis=0), vb, preferred_element_type=f32)
            o = jnp.zeros((QBLK, DOT_W), f32)
            st = jnp.zeros((QBLK, DOT_W), f32)
            for hh in range(HEADS_PER_DOT):
                sel = head_of_lane == hh
                o = jnp.where(sel, o_all[hh * QBLK:(hh + 1) * QBLK, :] * inv_den[hh], o)
                st = jnp.where(sel, lse[hh], st)
            o_ref[pl.ds(r0, QBLK), lanes] = o.astype(o_ref.dtype)
            st_ref[pl.ds(r0, QBLK), lanes] = st
        return carry

    lax.fori_loop(0, tq // QBLK, body, 0, unroll=4)


def _attention(qkv, *, dil, slopes, tq, name):
    bsz, _, sub, _ = qkv.shape
    w = ATT_GROUP_W
    nq = tq // RADIUS
    nblk = sub // RADIUS
    cur = lambda c: pl.BlockSpec((None, None, tq, w), lambda b, r, i: (b, r, i, c))
    prev = lambda c: pl.BlockSpec((None, None, RADIUS, w),
                                  lambda b, r, i: (b, r, jnp.maximum(i * nq - 1, 0), c))
    nxt = lambda c: pl.BlockSpec((None, None, RADIUS, w),
                                 lambda b, r, i: (b, r, jnp.minimum((i + 1) * nq, nblk - 1), c))
    out_spec = pl.BlockSpec((None, None, tq, w), lambda b, r, i: (b, r, i, 0))
    return pl.pallas_call(
        functools.partial(_attn_kernel, dil=dil, slopes=tuple(float(s) for s in slopes),
                          sub_len=sub, tq=tq),
        grid=(bsz, dil, sub // tq),
        in_specs=[cur(0), prev(1), cur(1), nxt(1), prev(2), cur(2), nxt(2)],
        out_specs=[out_spec, out_spec],
        out_shape=[jax.ShapeDtypeStruct((bsz, dil, sub, w), jnp.bfloat16),
                   jax.ShapeDtypeStruct((bsz, dil, sub, w), jnp.float32)],
        scratch_shapes=[pltpu.VMEM((tq + 2 * RADIUS, w), jnp.bfloat16),
                        pltpu.VMEM((tq + 2 * RADIUS, w), jnp.bfloat16),
                        pltpu.VMEM((3, HEADS_PER_GROUP, QBLK, KBLK), jnp.float32)],
        compiler_params=pltpu.CompilerParams(
            dimension_semantics=("arbitrary", "arbitrary", "arbitrary"),
            vmem_limit_bytes=VMEM_LIMIT),
        name=name,
    )(qkv, qkv, qkv, qkv, qkv, qkv, qkv)


def _dwconv3_rows(u, prev_row, next_row, w_ref):
    n = u.shape[0]
    row = lax.broadcasted_iota(jnp.int32, u.shape, 0)
    u_dn = jnp.where(row == 0, prev_row, pltpu.roll(u, 1, axis=0))
    u_up = jnp.where(row == n - 1, next_row, pltpu.roll(u, n - 1, axis=0))
    return u_dn * w_ref[0:1, :] + u * w_ref[1:2, :] + u_up * w_ref[2:3, :]


def _to_natural_order(src_ref, slab_ref):
    dil, n, w = src_ref.shape
    for r in range(dil):
        blk = src_ref[r].astype(jnp.float32)
        for c in range(w // LANES):
            slab_ref[c, pl.ds(r, n, stride=dil), :] = blk[:, c * LANES:(c + 1) * LANES]
    return jnp.concatenate([slab_ref[c] for c in range(w // LANES)], axis=-1)


def _mix_kernel(x_ref, xp_ref, xn_ref,
                o0_ref, o1_ref, o2_ref, s0_ref, s1_ref, s2_ref,
                wgb_ref, bgb_ref, wch_ref, bch_ref, wg_ref, bg_ref,
                cw_ref, wa_ref, wb_ref, wo_ref, bo_ref,
                g0_ref, b0_ref, g1_ref, b1_ref, out_ref,
                slab_o1, slab_s1, slab_o2, slab_s2, *, tm):
    i = pl.program_id(1)
    last = pl.num_programs(1) - 1
    f32 = jnp.float32
    x_all = jnp.concatenate([x_ref[...], xp_ref[...], xn_ref[...]], axis=0)
    h_all = _layer_norm(x_all, g0_ref[...], b0_ref[...])
    h = h_all[0:tm, :]
    lhs_all = h_all.astype(jnp.bfloat16)
    lhs = lhs_all[0:tm, :]

    ch = jnp.dot(lhs_all, wch_ref[...], preferred_element_type=f32) + bch_ref[...]
    u_all = ch[:, 0:D_CONV] * ch[:, D_CONV:]
    u = u_all[0:tm, :]
    u_prev = jnp.where(i > 0, u_all[tm + F32_ROWS - 1:tm + F32_ROWS, :], 0.0)
    u_next = jnp.where(i < last, u_all[tm + F32_ROWS:tm + F32_ROWS + 1, :], 0.0)
    conv = _dwconv3_rows(u, u_prev, u_next, cw_ref)
    gate_b = jnp.dot(lhs, wgb_ref[...], preferred_element_type=f32) + bgb_ref[...]
    y_a = jnp.dot((gate_b * conv).astype(jnp.bfloat16), wa_ref[...], preferred_element_type=f32)

    o0, s0 = o0_ref[0].astype(f32), s0_ref[0]
    o1, s1 = _to_natural_order(o1_ref, slab_o1), _to_natural_order(s1_ref, slab_s1)
    o2, s2 = _to_natural_order(o2_ref, slab_o2), _to_natural_order(s2_ref, slab_s2)
    m = jnp.maximum(jnp.maximum(s0, s1), s2)
    e0, e1, e2 = jnp.exp(s0 - m), jnp.exp(s1 - m), jnp.exp(s2 - m)
    comb = (e0 * o0 + e1 * o1 + e2 * o2) / (e0 + e1 + e2)
    y_b = jnp.dot(comb.astype(jnp.bfloat16), wb_ref[...], preferred_element_type=f32)

    gates = jnp.dot(lhs, wg_ref[...], preferred_element_type=f32) + bg_ref[...]
    merged = (jax.nn.sigmoid(gates[:, 0:D_MODEL]) * y_a
              + jax.nn.sigmoid(gates[:, D_MODEL:]) * y_b)
    mix = jnp.dot(merged.astype(jnp.bfloat16), wo_ref[...], preferred_element_type=f32) + bo_ref[...]
    out_ref[...] = _layer_norm(ALPHA * h + mix, g1_ref[...], b1_ref[...])


def _mix(x, outs, stats, weights, *, tm):
    bsz, seq, dm = x.shape
    nhalo = tm // F32_ROWS
    nrow_blk = seq // F32_ROWS
    att = lambda d: pl.BlockSpec((None, d, tm // d, ATT_GROUP_W), lambda b, i: (b, 0, i, 0))
    att_specs = [att(d) for _, d in GROUPS]
    slab = pltpu.VMEM((ATT_GROUP_W // LANES, tm, LANES), jnp.float32)
    return pl.pallas_call(
        functools.partial(_mix_kernel, tm=tm),
        grid=(bsz, seq // tm),
        in_specs=[pl.BlockSpec((None, tm, dm), lambda b, i: (b, i, 0)),
                  pl.BlockSpec((None, F32_ROWS, dm), lambda b, i: (b, jnp.maximum(i * nhalo - 1, 0), 0)),
                  pl.BlockSpec((None, F32_ROWS, dm),
                               lambda b, i: (b, jnp.minimum((i + 1) * nhalo, nrow_blk - 1), 0))]
                 + att_specs + att_specs + [_resident(a) for a in weights],
        out_specs=pl.BlockSpec((None, tm, dm), lambda b, i: (b, i, 0)),
        out_shape=jax.ShapeDtypeStruct((bsz, seq, dm), jnp.float32),
        scratch_shapes=[slab, slab, slab, slab],
        compiler_params=pltpu.CompilerParams(
            dimension_semantics=("parallel", "parallel"),
            vmem_limit_bytes=VMEM_LIMIT),
        name="mix",
    )(x, x, x, *outs, *stats, *weights)


def _ffn_kernel(h_ref, hp_ref, hn_ref, wu_ref, bu_ref, cw_ref, cb_ref, wd_ref, bd_ref,
                g_ref, b_ref, out_ref, f_ref, *, tm, chunks):
    i = pl.program_id(1)
    last = pl.num_programs(1) - 1
    f32 = jnp.float32
    h = h_ref[...]
    lhs_all = jnp.concatenate([h, hp_ref[...], hn_ref[...]], axis=0).astype(jnp.bfloat16)
    lhs = lhs_all[0:tm, :]
    lo = 0
    for ck in chunks:
        cols = slice(lo, lo + ck)
        gcols = slice(D_FF + lo, D_FF + lo + ck)
        a_all = jnp.dot(lhs_all, wu_ref[:, cols], preferred_element_type=f32) + bu_ref[:, cols]
        gate = jnp.dot(lhs, wu_ref[:, gcols], preferred_element_type=f32) + bu_ref[:, gcols]
        a = a_all[0:tm, :]
        a_prev = jnp.where(i > 0, a_all[tm + F32_ROWS - 1:tm + F32_ROWS, :], 0.0)
        a_next = jnp.where(i < last, a_all[tm + F32_ROWS:tm + F32_ROWS + 1, :], 0.0)
        pre = _dwconv3_rows(a, a_prev, a_next, cw_ref.at[:, cols]) + cb_ref[:, cols]
        f = 0.5 * pre * (1.0 + lax.erf(pre * (2.0 ** -0.5))) * gate
        f_ref[:, cols] = f.astype(jnp.bfloat16)
        lo += ck
    ffn = jnp.dot(f_ref[...], wd_ref[...], preferred_element_type=f32) + bd_ref[...]
    out_ref[...] = _layer_norm(ALPHA * h + ffn, g_ref[...], b_ref[...])


def _ffn(h1, wu, bu, cw, cb, wd, bd, g, b, *, tm, chunks):
    bsz, seq, dm = h1.shape
    assert sum(chunks) == D_FF
    nhalo = tm // F32_ROWS
    nrow_blk = seq // F32_ROWS
    weights = (wu, bu, cw, cb, wd, bd, g, b)
    return pl.pallas_call(
        functools.partial(_ffn_kernel, tm=tm, chunks=chunks),
        grid=(bsz, seq // tm),
        in_specs=[
            pl.BlockSpec((None, tm, dm), lambda bb, i: (bb, i, 0)),
            pl.BlockSpec((None, F32_ROWS, dm), lambda bb, i: (bb, jnp.maximum(i * nhalo - 1, 0), 0)),
            pl.BlockSpec((None, F32_ROWS, dm),
                         lambda bb, i: (bb, jnp.minimum((i + 1) * nhalo, nrow_blk - 1), 0)),
        ] + [_resident(a) for a in weights],
        out_specs=pl.BlockSpec((None, tm, dm), lambda bb, i: (bb, i, 0)),
        out_shape=jax.ShapeDtypeStruct((bsz, seq, dm), jnp.float32),
        scratch_shapes=[pltpu.VMEM((tm, D_FF), jnp.bfloat16)],
        compiler_params=pltpu.CompilerParams(
            dimension_semantics=("parallel", "parallel"),
            vmem_limit_bytes=VMEM_LIMIT),
        name="ffn",
    )(h1, h1, h1, *weights)


def kernel(x, ln0_g, ln0_b, w_in, b_in, conv_w, w_a, w_b, w_o, b_o, ln1_g, ln1_b,
           w_up, b_up, ffn_conv_w, ffn_conv_b, w_down, b_down, ln2_g, ln2_b):
    bf16 = jnp.bfloat16
    row = lambda v: v.reshape(1, -1)
    slopes = _alibi_slopes()
    g0, b0 = row(ln0_g), row(ln0_b)
    bsz, seq, _ = x.shape
    l = 0
    wi, bi = w_in[l], b_in[l]
    scale = HEAD_DIM ** -0.5
    w_qkv, b_qkv = [], []
    for g in range(N_GROUPS):
        lo, hi = g * ATT_GROUP_W, (g + 1) * ATT_GROUP_W
        w_qkv.append(jnp.concatenate([wi[:, OFF_Q + lo:OFF_Q + hi] * scale, wi[:, OFF_K + lo:OFF_K + hi],
                                      wi[:, OFF_V + lo:OFF_V + hi]], axis=1).astype(bf16))
        b_qkv.append(jnp.concatenate([bi[OFF_Q + lo:OFF_Q + hi] * scale, bi[OFF_K + lo:OFF_K + hi],
                                      bi[OFF_V + lo:OFF_V + hi]]).reshape(1, -1))
    qkvs = _qkv_proj(x, g0, b0, jnp.stack(w_qkv), jnp.stack(b_qkv), tm=512)
    outs, stats = [], []
    for g, (_, dil) in enumerate(GROUPS):
        o, st = _attention(qkvs[g], dil=dil, slopes=slopes[g], tq=min(1024, seq // dil), name=f"attn{g}")
        outs.append(o)
        stats.append(st)
    mix_weights = (
        wi[:, OFF_B:OFF_C].astype(bf16), row(bi[OFF_B:OFF_C]),
        wi[:, OFF_C:OFF_Q].astype(bf16), row(bi[OFF_C:OFF_Q]),
        wi[:, OFF_GA:].astype(bf16), row(bi[OFF_GA:]),
        conv_w[l], w_a[l].astype(bf16), w_b[l].astype(bf16), w_o[l].astype(bf16), row(b_o[l]),
        g0, b0, row(ln1_g[l]), row(ln1_b[l]))
    h1 = _mix(x, outs, stats, mix_weights, tm=512)
    return _ffn(h1, w_up[l].astype(bf16), row(b_up[l]), ffn_conv_w[l], row(ffn_conv_b[l]),
                w_down[l].astype(bf16), row(b_down[l]), row(ln2_g[l]), row(ln2_b[l]),
                tm=512, chunks=_chunks(D_FF, 2 * MXU_TILE))
```

```python
import functools

import numpy as np
import jax
import jax.numpy as jnp
from jax import lax
from jax.experimental import pallas as pl
from jax.experimental.pallas import tpu as pltpu

D_MODEL = 1024
D_CONV = D_MODEL
HEAD_DIM = 64
HEADS_PER_GROUP = 8
GROUPS = ((128, 1), (512, 4), (2048, 16))
N_GROUPS = len(GROUPS)
N_ATT_HEADS = N_GROUPS * HEADS_PER_GROUP
ATT_GROUP_W = HEADS_PER_GROUP * HEAD_DIM
ATT_QKV_W = N_GROUPS * ATT_GROUP_W
D_FF = 2816
LN_EPS = 1e-5
DEPTH = 1
assert DEPTH == 1
ALPHA = (2.0 * DEPTH) ** 0.25

OFF_B = 0
OFF_C = OFF_B + D_CONV
OFF_H = OFF_C + D_CONV
OFF_Q = OFF_H + D_CONV
OFF_K = OFF_Q + ATT_QKV_W
OFF_V = OFF_K + ATT_QKV_W
OFF_GA = OFF_V + ATT_QKV_W
OFF_GB = OFF_GA + D_MODEL

RADIUS = 64
QBLK = 2 * RADIUS
KBLK = QBLK + 2 * RADIUS
LANES = 128
BF16_ROWS = 16
F32_ROWS = 8
HEADS_PER_DOT = 4
DOT_W = HEADS_PER_DOT * HEAD_DIM
MXU_TILE = 256
VMEM_LIMIT = 56 * 1024 * 1024

assert all(w // (2 * d) == RADIUS for w, d in GROUPS)
assert D_FF % MXU_TILE == 0


def _chunks(total, width):
    full = [width] * (total // width)
    rem = total - sum(full)
    return tuple(full + ([rem] if rem else []))


def _layer_norm(xf, g, b):
    mu = jnp.mean(xf, -1, keepdims=True)
    xc = xf - mu
    var = jnp.mean(xc * xc, -1, keepdims=True)
    return xc * lax.rsqrt(var + LN_EPS) * g + b


def _alibi_slopes():
    n = N_ATT_HEADS
    return np.exp2(-8.0 * np.arange(1, n + 1, dtype=np.float64) / n).reshape(N_GROUPS, HEADS_PER_GROUP)


def _resident(a):
    return pl.BlockSpec(a.shape, lambda b, i: (0,) * a.ndim, pipeline_mode=pl.Buffered(1))


def _qkv_proj_kernel(x_ref, g_ref, b_ref, w_ref, bias_ref, *refs, tm):
    o_refs, (hb_ref, slab_ref) = refs[:N_GROUPS], refs[N_GROUPS:]
    h = _layer_norm(x_ref[...], g_ref[...], b_ref[...])
    n_slabs = D_MODEL // LANES
    for c in range(n_slabs):
        slab_ref[c] = h[:, c * LANES:(c + 1) * LANES]
    for g, (_, dil) in enumerate(GROUPS):
        n = tm // dil
        if dil == 1:
            hb_ref[g] = h.astype(jnp.bfloat16)
        else:
            for r in range(dil):
                for c in range(n_slabs):
                    hb_ref[g, r * n:(r + 1) * n, c * LANES:(c + 1) * LANES] = (
                        slab_ref[c, pl.ds(r, n, stride=dil), :].astype(jnp.bfloat16))
        acc = jnp.dot(hb_ref[g], w_ref[g], preferred_element_type=jnp.float32) + bias_ref[g]
        for r in range(dil):
            o_refs[g][r] = acc[r * n:(r + 1) * n, :].astype(o_refs[g].dtype)


def _qkv_proj(x, ln_g, ln_b, w, bias, *, tm):
    bsz, seq, dm = x.shape
    n = w.shape[-1]
    return pl.pallas_call(
        functools.partial(_qkv_proj_kernel, tm=tm),
        grid=(bsz, seq // tm),
        in_specs=[pl.BlockSpec((None, tm, dm), lambda b, i: (b, i, 0)),
                  _resident(ln_g), _resident(ln_b), _resident(w), _resident(bias)],
        out_specs=[pl.BlockSpec((None, d, tm // d, n), lambda b, i: (b, 0, i, 0)) for _, d in GROUPS],
        out_shape=[jax.ShapeDtypeStruct((bsz, d, seq // d, n), jnp.bfloat16) for _, d in GROUPS],
        scratch_shapes=[pltpu.VMEM((N_GROUPS, tm, dm), jnp.bfloat16),
                        pltpu.VMEM((dm // LANES, tm, LANES), jnp.float32)],
        compiler_params=pltpu.CompilerParams(
            dimension_semantics=("parallel", "parallel"),
            vmem_limit_bytes=VMEM_LIMIT),
        name="proj_qkv",
    )(x, ln_g, ln_b, w, bias)


def _attn_kernel(q_ref, kp_ref, kc_ref, kn_ref, vp_ref, vc_ref, vn_ref, o_ref, st_ref,
                 bias_ref, *, dil, slopes, sub_len, tq):
    i = pl.program_id(2)
    nq = tq // QBLK

    def window(prev_ref, cur_ref, next_ref, j, lanes):
        lo, hi = j * QBLK - RADIUS, (j + 1) * QBLK + RADIUS
        parts = []
        if lo < 0:
            parts.append(prev_ref[:, lanes])
        parts.append(cur_ref[max(lo, 0):min(hi, tq), lanes])
        if hi > tq:
            parts.append(next_ref[:, lanes])
        return parts[0] if len(parts) == 1 else jnp.concatenate(parts, axis=0)

    @pl.when((pl.program_id(0) == 0) & (pl.program_id(1) == 0) & (i == 0))
    def _():
        row = lax.broadcasted_iota(jnp.int32, (QBLK, KBLK), 0)
        col = lax.broadcasted_iota(jnp.int32, (QBLK, KBLK), 1)
        rel = jnp.abs(col - RADIUS - row)
        band = rel <= RADIUS
        dist = (rel * dil).astype(jnp.float32)
        for var, ok in enumerate((band, band & (col >= RADIUS), band & (col < KBLK - RADIUS))):
            for h in range(HEADS_PER_GROUP):
                bias_ref[var, h] = jnp.where(ok, -slopes[h] * dist, -jnp.inf)

    head_of_lane = lax.broadcasted_iota(jnp.int32, (1, DOT_W), 1) // HEAD_DIM
    f32 = jnp.float32
    last_blk = sub_len // QBLK - 1

    for j in range(nq):
        rows = slice(j * QBLK, (j + 1) * QBLK)
        blk = i * nq + j
        var = jnp.where(blk == 0, 1, jnp.where(blk == last_blk, 2, 0))
        for part in range(HEADS_PER_GROUP // HEADS_PER_DOT):
            lanes = slice(part * DOT_W, (part + 1) * DOT_W)
            qb = q_ref[rows, lanes]
            kb = window(kp_ref, kc_ref, kn_ref, j, lanes)
            vb = window(vp_ref, vc_ref, vn_ref, j, lanes)
            q_heads = jnp.concatenate(
                [jnp.where(head_of_lane == hh, qb, jnp.zeros_like(qb)) for hh in range(HEADS_PER_DOT)],
                axis=0)
            s_all = lax.dot_general(q_heads, kb, (((1,), (1,)), ((), ())), preferred_element_type=f32)
            p_heads, inv_den, lse = [], [], []
            for hh in range(HEADS_PER_DOT):
                s = s_all[hh * QBLK:(hh + 1) * QBLK, :] + bias_ref[var, part * HEADS_PER_DOT + hh]
                m = jnp.max(s, -1, keepdims=True)
                p = jnp.exp(s - m)
                den = jnp.sum(p, -1, keepdims=True)
                p_heads.append(p.astype(jnp.bfloat16))
                inv_den.append(1.0 / den)
                lse.append(m + jnp.log(den))
            o = jnp.zeros((QBLK, DOT_W), f32)
            st = jnp.zeros((QBLK, DOT_W), f32)
            for hh in range(HEADS_PER_DOT):
                sel = head_of_lane == hh
                o_hh = jnp.dot(p_heads[hh], vb, preferred_element_type=f32)
                o = jnp.where(sel, o_hh * inv_den[hh], o)
                st = jnp.where(sel, lse[hh], st)
            o_ref[rows, lanes] = o.astype(o_ref.dtype)
            st_ref[rows, lanes] = st


def _attention(qkv, *, dil, slopes, tq, name):
    bsz, _, sub, _ = qkv.shape
    w = ATT_GROUP_W
    nq = tq // RADIUS
    nblk = sub // RADIUS
    cur = lambda c: pl.BlockSpec((None, None, tq, w), lambda b, r, i: (b, r, i, c))
    prev = lambda c: pl.BlockSpec((None, None, RADIUS, w),
                                  lambda b, r, i: (b, r, jnp.maximum(i * nq - 1, 0), c))
    nxt = lambda c: pl.BlockSpec((None, None, RADIUS, w),
                                 lambda b, r, i: (b, r, jnp.minimum((i + 1) * nq, nblk - 1), c))
    out_spec = pl.BlockSpec((None, None, tq, w), lambda b, r, i: (b, r, i, 0))
    return pl.pallas_call(
        functools.partial(_attn_kernel, dil=dil, slopes=tuple(float(s) for s in slopes),
                          sub_len=sub, tq=tq),
        grid=(bsz, dil, sub // tq),
        in_specs=[cur(0), prev(1), cur(1), nxt(1), prev(2), cur(2), nxt(2)],
        out_specs=[out_spec, out_spec],
        out_shape=[jax.ShapeDtypeStruct((bsz, dil, sub, w), jnp.bfloat16),
                   jax.ShapeDtypeStruct((bsz, dil, sub, w), jnp.float32)],
        scratch_shapes=[pltpu.VMEM((3, HEADS_PER_GROUP, QBLK, KBLK), jnp.float32)],
        compiler_params=pltpu.CompilerParams(
            dimension_semantics=("arbitrary", "arbitrary", "arbitrary"),
            vmem_limit_bytes=VMEM_LIMIT),
        name=name,
    )(qkv, qkv, qkv, qkv, qkv, qkv, qkv)


def _dwconv3_rows(u, prev_row, next_row, w_ref):
    n = u.shape[0]
    row = lax.broadcasted_iota(jnp.int32, u.shape, 0)
    u_dn = jnp.where(row == 0, prev_row, pltpu.roll(u, 1, axis=0))
    u_up = jnp.where(row == n - 1, next_row, pltpu.roll(u, n - 1, axis=0))
    return u_dn * w_ref[0:1, :] + u * w_ref[1:2, :] + u_up * w_ref[2:3, :]


def _to_natural_order(src_ref, slab_ref):
    dil, n, w = src_ref.shape
    for r in range(dil):
        blk = src_ref[r].astype(jnp.float32)
        for c in range(w // LANES):
            slab_ref[c, pl.ds(r, n, stride=dil), :] = blk[:, c * LANES:(c + 1) * LANES]
    return jnp.concatenate([slab_ref[c] for c in range(w // LANES)], axis=-1)


def _mix_kernel(x_ref, xp_ref, xn_ref,
                o0_ref, o1_ref, o2_ref, s0_ref, s1_ref, s2_ref,
                wgb_ref, bgb_ref, wch_ref, bch_ref, wg_ref, bg_ref,
                cw_ref, wa_ref, wb_ref, wo_ref, bo_ref,
                g0_ref, b0_ref, g1_ref, b1_ref, out_ref,
                slab_o1, slab_s1, slab_o2, slab_s2, *, tm):
    i = pl.program_id(1)
    last = pl.num_programs(1) - 1
    f32 = jnp.float32
    x_all = jnp.concatenate([x_ref[...], xp_ref[...], xn_ref[...]], axis=0)
    h_all = _layer_norm(x_all, g0_ref[...], b0_ref[...])
    h = h_all[0:tm, :]
    lhs_all = h_all.astype(jnp.bfloat16)
    lhs = lhs_all[0:tm, :]

    ch = jnp.dot(lhs_all, wch_ref[...], preferred_element_type=f32) + bch_ref[...]
    u_all = ch[:, 0:D_CONV] * ch[:, D_CONV:]
    u = u_all[0:tm, :]
    u_prev = jnp.where(i > 0, u_all[tm + F32_ROWS - 1:tm + F32_ROWS, :], 0.0)
    u_next = jnp.where(i < last, u_all[tm + F32_ROWS:tm + F32_ROWS + 1, :], 0.0)
    conv = _dwconv3_rows(u, u_prev, u_next, cw_ref)
    gate_b = jnp.dot(lhs, wgb_ref[...], preferred_element_type=f32) + bgb_ref[...]
    y_a = jnp.dot((gate_b * conv).astype(jnp.bfloat16), wa_ref[...], preferred_element_type=f32)

    o0, s0 = o0_ref[0].astype(f32), s0_ref[0]
    o1, s1 = _to_natural_order(o1_ref, slab_o1), _to_natural_order(s1_ref, slab_s1)
    o2, s2 = _to_natural_order(o2_ref, slab_o2), _to_natural_order(s2_ref, slab_s2)
    m = jnp.maximum(jnp.maximum(s0, s1), s2)
    e0, e1, e2 = jnp.exp(s0 - m), jnp.exp(s1 - m), jnp.exp(s2 - m)
    comb = (e0 * o0 + e1 * o1 + e2 * o2) / (e0 + e1 + e2)
    y_b = jnp.dot(comb.astype(jnp.bfloat16), wb_ref[...], preferred_element_type=f32)

    gates = jnp.dot(lhs, wg_ref[...], preferred_element_type=f32) + bg_ref[...]
    merged = (jax.nn.sigmoid(gates[:, 0:D_MODEL]) * y_a
              + jax.nn.sigmoid(gates[:, D_MODEL:]) * y_b)
    mix = jnp.dot(merged.astype(jnp.bfloat16), wo_ref[...], preferred_element_type=f32) + bo_ref[...]
    out_ref[...] = _layer_norm(ALPHA * h + mix, g1_ref[...], b1_ref[...])


def _mix(x, outs, stats, weights, *, tm):
    bsz, seq, dm = x.shape
    nhalo = tm // F32_ROWS
    nrow_blk = seq // F32_ROWS
    att = lambda d: pl.BlockSpec((None, d, tm // d, ATT_GROUP_W), lambda b, i: (b, 0, i, 0))
    att_specs = [att(d) for _, d in GROUPS]
    slab = pltpu.VMEM((ATT_GROUP_W // LANES, tm, LANES), jnp.float32)
    return pl.pallas_call(
        functools.partial(_mix_kernel, tm=tm),
        grid=(bsz, seq // tm),
        in_specs=[pl.BlockSpec((None, tm, dm), lambda b, i: (b, i, 0)),
                  pl.BlockSpec((None, F32_ROWS, dm), lambda b, i: (b, jnp.maximum(i * nhalo - 1, 0), 0)),
                  pl.BlockSpec((None, F32_ROWS, dm),
                               lambda b, i: (b, jnp.minimum((i + 1) * nhalo, nrow_blk - 1), 0))]
                 + att_specs + att_specs + [_resident(a) for a in weights],
        out_specs=pl.BlockSpec((None, tm, dm), lambda b, i: (b, i, 0)),
        out_shape=jax.ShapeDtypeStruct((bsz, seq, dm), jnp.float32),
        scratch_shapes=[slab, slab, slab, slab],
        compiler_params=pltpu.CompilerParams(
            dimension_semantics=("parallel", "parallel"),
            vmem_limit_bytes=VMEM_LIMIT),
        name="mix",
    )(x, x, x, *outs, *stats, *weights)


def _ffn_kernel(h_ref, hp_ref, hn_ref, wu_ref, bu_ref, cw_ref, cb_ref, wd_ref, bd_ref,
                g_ref, b_ref, out_ref, f_ref, *, tm, chunks):
    i = pl.program_id(1)
    last = pl.num_programs(1) - 1
    f32 = jnp.float32
    h = h_ref[...]
    lhs_all = jnp.concatenate([h, hp_ref[...], hn_ref[...]], axis=0).astype(jnp.bfloat16)
    lhs = lhs_all[0:tm, :]
    lo = 0
    for ck in chunks:
        cols = slice(lo, lo + ck)
        gcols = slice(D_FF + lo, D_FF + lo + ck)
        a_all = jnp.dot(lhs_all, wu_ref[:, cols], preferred_element_type=f32) + bu_ref[:, cols]
        gate = jnp.dot(lhs, wu_ref[:, gcols], preferred_element_type=f32) + bu_ref[:, gcols]
        a = a_all[0:tm, :]
        a_prev = jnp.where(i > 0, a_all[tm + F32_ROWS - 1:tm + F32_ROWS, :], 0.0)
        a_next = jnp.where(i < last, a_all[tm + F32_ROWS:tm + F32_ROWS + 1, :], 0.0)
        pre = _dwconv3_rows(a, a_prev, a_next, cw_ref.at[:, cols]) + cb_ref[:, cols]
        f = 0.5 * pre * (1.0 + lax.erf(pre * (2.0 ** -0.5))) * gate
        f_ref[:, cols] = f.astype(jnp.bfloat16)
        lo += ck
    ffn = jnp.dot(f_ref[...], wd_ref[...], preferred_element_type=f32) + bd_ref[...]
    out_ref[...] = _layer_norm(ALPHA * h + ffn, g_ref[...], b_ref[...])


def _ffn(h1, wu, bu, cw, cb, wd, bd, g, b, *, tm, chunks):
    bsz, seq, dm = h1.shape
    assert sum(chunks) == D_FF
    nhalo = tm // F32_ROWS
    nrow_blk = seq // F32_ROWS
    weights = (wu, bu, cw, cb, wd, bd, g, b)
    return pl.pallas_call(
        functools.partial(_ffn_kernel, tm=tm, chunks=chunks),
        grid=(bsz, seq // tm),
        in_specs=[
            pl.BlockSpec((None, tm, dm), lambda bb, i: (bb, i, 0)),
            pl.BlockSpec((None, F32_ROWS, dm), lambda bb, i: (bb, jnp.maximum(i * nhalo - 1, 0), 0)),
            pl.BlockSpec((None, F32_ROWS, dm),
                         lambda bb, i: (bb, jnp.minimum((i + 1) * nhalo, nrow_blk - 1), 0)),
        ] + [_resident(a) for a in weights],
        out_specs=pl.BlockSpec((None, tm, dm), lambda bb, i: (bb, i, 0)),
        out_shape=jax.ShapeDtypeStruct((bsz, seq, dm), jnp.float32),
        scratch_shapes=[pltpu.VMEM((tm, D_FF), jnp.bfloat16)],
        compiler_params=pltpu.CompilerParams(
            dimension_semantics=("parallel", "parallel"),
            vmem_limit_bytes=VMEM_LIMIT),
        name="ffn",
    )(h1, h1, h1, *weights)


def kernel(x, ln0_g, ln0_b, w_in, b_in, conv_w, w_a, w_b, w_o, b_o, ln1_g, ln1_b,
           w_up, b_up, ffn_conv_w, ffn_conv_b, w_down, b_down, ln2_g, ln2_b):
    bf16 = jnp.bfloat16
    row = lambda v: v.reshape(1, -1)
    slopes = _alibi_slopes()
    g0, b0 = row(ln0_g), row(ln0_b)
    bsz, seq, _ = x.shape
    l = 0
    wi, bi = w_in[l], b_in[l]
    scale = HEAD_DIM ** -0.5
    w_qkv, b_qkv = [], []
    for g in range(N_GROUPS):
        lo, hi = g * ATT_GROUP_W, (g + 1) * ATT_GROUP_W
        w_qkv.append(jnp.concatenate([wi[:, OFF_Q + lo:OFF_Q + hi] * scale, wi[:, OFF_K + lo:OFF_K + hi],
                                      wi[:, OFF_V + lo:OFF_V + hi]], axis=1).astype(bf16))
        b_qkv.append(jnp.concatenate([bi[OFF_Q + lo:OFF_Q + hi] * scale, bi[OFF_K + lo:OFF_K + hi],
                                      bi[OFF_V + lo:OFF_V + hi]]).reshape(1, -1))
    qkvs = _qkv_proj(x, g0, b0, jnp.stack(w_qkv), jnp.stack(b_qkv), tm=512)
    outs, stats = [], []
    for g, (_, dil) in enumerate(GROUPS):
        o, st = _attention(qkvs[g], dil=dil, slopes=slopes[g], tq=min(1024, seq // dil), name=f"attn{g}")
        outs.append(o)
        stats.append(st)
    mix_weights = (
        wi[:, OFF_B:OFF_C].astype(bf16), row(bi[OFF_B:OFF_C]),
        wi[:, OFF_C:OFF_Q].astype(bf16), row(bi[OFF_C:OFF_Q]),
        wi[:, OFF_GA:].astype(bf16), row(bi[OFF_GA:]),
        conv_w[l], w_a[l].astype(bf16), w_b[l].astype(bf16), w_o[l].astype(bf16), row(b_o[l]),
        g0, b0, row(ln1_g[l]), row(ln1_b[l]))
    h1 = _mix(x, outs, stats, mix_weights, tm=512)
    return _ffn(h1, w_up[l].astype(bf16), row(b_up[l]), ffn_conv_w[l], row(ffn_conv_b[l]),
                w_down[l].astype(bf16), row(b_down[l]), row(ln2_g[l]), row(ln2_b[l]),
                tm=1024, chunks=_chunks(D_FF, 2 * MXU_TILE))
```

```python
import functools

import numpy as np
import jax
import jax.numpy as jnp
from jax import lax
from jax.experimental import pallas as pl
from jax.experimental.pallas import tpu as pltpu

D_MODEL = 1024
D_CONV = D_MODEL
HEAD_DIM = 64
HEADS_PER_GROUP = 8
GROUPS = ((128, 1), (512, 4), (2048, 16))
N_GROUPS = len(GROUPS)
N_ATT_HEADS = N_GROUPS * HEADS_PER_GROUP
ATT_GROUP_W = HEADS_PER_GROUP * HEAD_DIM
ATT_QKV_W = N_GROUPS * ATT_GROUP_W
D_FF = 2816
LN_EPS = 1e-5
DEPTH = 1
assert DEPTH == 1
ALPHA = (2.0 * DEPTH) ** 0.25

OFF_B = 0
OFF_C = OFF_B + D_CONV
OFF_H = OFF_C + D_CONV
OFF_Q = OFF_H + D_CONV
OFF_K = OFF_Q + ATT_QKV_W
OFF_V = OFF_K + ATT_QKV_W
OFF_GA = OFF_V + ATT_QKV_W
OFF_GB = OFF_GA + D_MODEL

RADIUS = 64
QBLK = 2 * RADIUS
KBLK = QBLK + 2 * RADIUS
LANES = 128
BF16_ROWS = 16
F32_ROWS = 8
HEADS_PER_DOT = 4
DOT_W = HEADS_PER_DOT * HEAD_DIM
MXU_TILE = 256
VMEM_LIMIT = 56 * 1024 * 1024

assert all(w // (2 * d) == RADIUS for w, d in GROUPS)
assert D_FF % MXU_TILE == 0


def _chunks(total, width):
    full = [width] * (total // width)
    rem = total - sum(full)
    return tuple(full + ([rem] if rem else []))


def _layer_norm(xf, g, b):
    mu = jnp.mean(xf, -1, keepdims=True)
    xc = xf - mu
    var = jnp.mean(xc * xc, -1, keepdims=True)
    return xc * lax.rsqrt(var + LN_EPS) * g + b


def _alibi_slopes():
    n = N_ATT_HEADS
    return np.exp2(-8.0 * np.arange(1, n + 1, dtype=np.float64) / n).reshape(N_GROUPS, HEADS_PER_GROUP)


def _resident(a):
    return pl.BlockSpec(a.shape, lambda b, i: (0,) * a.ndim, pipeline_mode=pl.Buffered(1))


def _resident_cols(a, offset, width):
    assert offset % width == 0
    return pl.BlockSpec((a.shape[0], width), lambda b, i: (0, offset // width),
                        pipeline_mode=pl.Buffered(1))


def _qkv_proj_kernel(x_ref, g_ref, b_ref, *refs, tm):
    n_w = 3 * N_GROUPS
    w_refs, bias_refs = refs[:n_w], refs[n_w:2 * n_w]
    o_refs = refs[2 * n_w:2 * n_w + N_GROUPS]
    h_ref, hb_out_ref, hb_ref, slab_ref = refs[2 * n_w + N_GROUPS:]
    h = _layer_norm(x_ref[...], g_ref[...], b_ref[...])
    hb = h.astype(jnp.bfloat16)
    h_ref[...] = h
    hb_out_ref[...] = hb
    n_slabs = D_MODEL // LANES
    for c in range(n_slabs):
        slab_ref[c] = h[:, c * LANES:(c + 1) * LANES]
    for g, (_, dil) in enumerate(GROUPS):
        n = tm // dil
        if dil == 1:
            lhs = hb
        else:
            for r in range(dil):
                for c in range(n_slabs):
                    hb_ref[g, r * n:(r + 1) * n, c * LANES:(c + 1) * LANES] = (
                        slab_ref[c, pl.ds(r, n, stride=dil), :].astype(jnp.bfloat16))
            lhs = hb_ref[g]
        for t in range(3):
            acc = (jnp.dot(lhs, w_refs[3 * g + t][...], preferred_element_type=jnp.float32)
                   + bias_refs[3 * g + t][...])
            if t == 0:
                acc = acc * HEAD_DIM ** -0.5
            cols = slice(t * ATT_GROUP_W, (t + 1) * ATT_GROUP_W)
            for r in range(dil):
                o_refs[g][r, :, cols] = acc[r * n:(r + 1) * n, :].astype(o_refs[g].dtype)


def _qkv_proj(x, ln_g, ln_b, w_in, b_in, *, tm):
    bsz, seq, dm = x.shape
    n = 3 * ATT_GROUP_W
    offsets = [off + g * ATT_GROUP_W for g in range(N_GROUPS) for off in (OFF_Q, OFF_K, OFF_V)]
    tile = pl.BlockSpec((None, tm, dm), lambda b, i: (b, i, 0))
    return pl.pallas_call(
        functools.partial(_qkv_proj_kernel, tm=tm),
        grid=(bsz, seq // tm),
        in_specs=[tile, _resident(ln_g), _resident(ln_b)]
                 + [_resident_cols(w_in, off, ATT_GROUP_W) for off in offsets]
                 + [_resident_cols(b_in, off, ATT_GROUP_W) for off in offsets],
        out_specs=[pl.BlockSpec((None, d, tm // d, n), lambda b, i: (b, 0, i, 0)) for _, d in GROUPS]
                  + [tile, tile],
        out_shape=[jax.ShapeDtypeStruct((bsz, d, seq // d, n), jnp.bfloat16) for _, d in GROUPS]
                  + [jax.ShapeDtypeStruct((bsz, seq, dm), jnp.float32),
                     jax.ShapeDtypeStruct((bsz, seq, dm), jnp.bfloat16)],
        scratch_shapes=[pltpu.VMEM((N_GROUPS, tm, dm), jnp.bfloat16),
                        pltpu.VMEM((dm // LANES, tm, LANES), jnp.float32)],
        compiler_params=pltpu.CompilerParams(
            dimension_semantics=("parallel", "parallel"),
            vmem_limit_bytes=VMEM_LIMIT),
        name="proj_qkv",
    )(x, ln_g, ln_b, *([w_in] * len(offsets)), *([b_in] * len(offsets)))


def _attn_kernel(q_ref, kp_ref, kc_ref, kn_ref, vp_ref, vc_ref, vn_ref, o_ref, st_ref,
                 bias_ref, *, dil, slopes, sub_len, tq):
    i = pl.program_id(2)
    nq = tq // QBLK

    def window(prev_ref, cur_ref, next_ref, j, lanes):
        lo, hi = j * QBLK - RADIUS, (j + 1) * QBLK + RADIUS
        parts = []
        if lo < 0:
            parts.append(prev_ref[:, lanes])
        parts.append(cur_ref[max(lo, 0):min(hi, tq), lanes])
        if hi > tq:
            parts.append(next_ref[:, lanes])
        return parts[0] if len(parts) == 1 else jnp.concatenate(parts, axis=0)

    @pl.when((pl.program_id(0) == 0) & (pl.program_id(1) == 0) & (i == 0))
    def _():
        row = lax.broadcasted_iota(jnp.int32, (QBLK, KBLK), 0)
        col = lax.broadcasted_iota(jnp.int32, (QBLK, KBLK), 1)
        rel = jnp.abs(col - RADIUS - row)
        band = rel <= RADIUS
        dist = (rel * dil).astype(jnp.float32)
        for var, ok in enumerate((band, band & (col >= RADIUS), band & (col < KBLK - RADIUS))):
            for h in range(HEADS_PER_GROUP):
                bias_ref[var, h] = jnp.where(ok, -slopes[h] * dist, -jnp.inf)

    head_of_lane = lax.broadcasted_iota(jnp.int32, (1, DOT_W), 1) // HEAD_DIM
    f32 = jnp.float32
    last_blk = sub_len // QBLK - 1

    for j in range(nq):
        rows = slice(j * QBLK, (j + 1) * QBLK)
        blk = i * nq + j
        var = jnp.where(blk == 0, 1, jnp.where(blk == last_blk, 2, 0))
        for part in range(HEADS_PER_GROUP // HEADS_PER_DOT):
            lanes = slice(part * DOT_W, (part + 1) * DOT_W)
            qb = q_ref[rows, lanes]
            kb = window(kp_ref, kc_ref, kn_ref, j, lanes)
            vb = window(vp_ref, vc_ref, vn_ref, j, lanes)
            q_heads = jnp.concatenate(
                [jnp.where(head_of_lane == hh, qb, jnp.zeros_like(qb)) for hh in range(HEADS_PER_DOT)],
                axis=0)
            s_all = lax.dot_general(q_heads, kb, (((1,), (1,)), ((), ())), preferred_element_type=f32)
            p_heads, inv_den, lse = [], [], []
            for hh in range(HEADS_PER_DOT):
                s = s_all[hh * QBLK:(hh + 1) * QBLK, :] + bias_ref[var, part * HEADS_PER_DOT + hh]
                m = jnp.max(s, -1, keepdims=True)
                p = jnp.exp(s - m)
                den = jnp.sum(p, -1, keepdims=True)
                p_heads.append(p.astype(jnp.bfloat16))
                inv_den.append(1.0 / den)
                lse.append(m + jnp.log(den))
            o = jnp.zeros((QBLK, DOT_W), f32)
            st = jnp.zeros((QBLK, DOT_W), f32)
            for hh in range(HEADS_PER_DOT):
                sel = head_of_lane == hh
                o_hh = jnp.dot(p_heads[hh], vb, preferred_element_type=f32)
                o = jnp.where(sel, o_hh * inv_den[hh], o)
                st = jnp.where(sel, lse[hh], st)
            o_ref[rows, lanes] = o.astype(o_ref.dtype)
            st_ref[rows, lanes] = st


def _attention(qkv, *, dil, slopes, tq, name):
    bsz, _, sub, _ = qkv.shape
    w = ATT_GROUP_W
    nq = tq // RADIUS
    nblk = sub // RADIUS
    cur = lambda c: pl.BlockSpec((None, None, tq, w), lambda b, r, i: (b, r, i, c))
    prev = lambda c: pl.BlockSpec((None, None, RADIUS, w),
                                  lambda b, r, i: (b, r, jnp.maximum(i * nq - 1, 0), c))
    nxt = lambda c: pl.BlockSpec((None, None, RADIUS, w),
                                 lambda b, r, i: (b, r, jnp.minimum((i + 1) * nq, nblk - 1), c))
    out_spec = pl.BlockSpec((None, None, tq, w), lambda b, r, i: (b, r, i, 0))
    return pl.pallas_call(
        functools.partial(_attn_kernel, dil=dil, slopes=tuple(float(s) for s in slopes),
                          sub_len=sub, tq=tq),
        grid=(bsz, dil, sub // tq),
        in_specs=[cur(0), prev(1), cur(1), nxt(1), prev(2), cur(2), nxt(2)],
        out_specs=[out_spec, out_spec],
        out_shape=[jax.ShapeDtypeStruct((bsz, dil, sub, w), jnp.bfloat16),
                   jax.ShapeDtypeStruct((bsz, dil, sub, w), jnp.float32)],
        scratch_shapes=[pltpu.VMEM((3, HEADS_PER_GROUP, QBLK, KBLK), jnp.float32)],
        compiler_params=pltpu.CompilerParams(
            dimension_semantics=("arbitrary", "arbitrary", "arbitrary"),
            vmem_limit_bytes=VMEM_LIMIT),
        name=name,
    )(qkv, qkv, qkv, qkv, qkv, qkv, qkv)


def _dwconv3_rows(u, prev_row, next_row, w_ref):
    n = u.shape[0]
    row = lax.broadcasted_iota(jnp.int32, u.shape, 0)
    u_dn = jnp.where(row == 0, prev_row, pltpu.roll(u, 1, axis=0))
    u_up = jnp.where(row == n - 1, next_row, pltpu.roll(u, n - 1, axis=0))
    return u_dn * w_ref[0:1, :] + u * w_ref[1:2, :] + u_up * w_ref[2:3, :]


def _to_natural_order(src_ref, slab_ref):
    dil, n, w = src_ref.shape
    for r in range(dil):
        blk = src_ref[r].astype(jnp.float32)
        for c in range(w // LANES):
            slab_ref[c, pl.ds(r, n, stride=dil), :] = blk[:, c * LANES:(c + 1) * LANES]
    return jnp.concatenate([slab_ref[c] for c in range(w // LANES)], axis=-1)


def _mix_kernel(h_ref, hb_ref, hbp_ref, hbn_ref,
                o0_ref, o1_ref, o2_ref, s0_ref, s1_ref, s2_ref,
                wgb_ref, wc_ref, wh_ref, wga0_ref, wga1_ref, wgb0_ref, wgb1_ref,
                bgb_ref, bc_ref, bh_ref, bga0_ref, bga1_ref, bgb0_ref, bgb1_ref,
                cw_ref, wa_ref, wb_ref, wo_ref, bo_ref, g1_ref, b1_ref, out_ref,
                slab_o1, slab_s1, slab_o2, slab_s2, *, tm):
    i = pl.program_id(1)
    last = pl.num_programs(1) - 1
    f32, bf16 = jnp.float32, jnp.bfloat16
    lhs = hb_ref[...]
    lhs_all = jnp.concatenate([lhs, hbp_ref[...], hbn_ref[...]], axis=0)
    dot = functools.partial(jnp.dot, preferred_element_type=f32)

    u_all = (dot(lhs_all, wc_ref[...]) + bc_ref[...]) * (dot(lhs_all, wh_ref[...]) + bh_ref[...])
    u = u_all[0:tm, :]
    u_prev = jnp.where(i > 0, u_all[tm + BF16_ROWS - 1:tm + BF16_ROWS, :], 0.0)
    u_next = jnp.where(i < last, u_all[tm + BF16_ROWS:tm + BF16_ROWS + 1, :], 0.0)
    conv = _dwconv3_rows(u, u_prev, u_next, cw_ref)
    gate_b = dot(lhs, wgb_ref[...]) + bgb_ref[...]
    y_a = dot((gate_b * conv).astype(bf16), wa_ref[...])

    o0, s0 = o0_ref[0].astype(f32), s0_ref[0]
    o1, s1 = _to_natural_order(o1_ref, slab_o1), _to_natural_order(s1_ref, slab_s1)
    o2, s2 = _to_natural_order(o2_ref, slab_o2), _to_natural_order(s2_ref, slab_s2)
    m = jnp.maximum(jnp.maximum(s0, s1), s2)
    e0, e1, e2 = jnp.exp(s0 - m), jnp.exp(s1 - m), jnp.exp(s2 - m)
    comb = (e0 * o0 + e1 * o1 + e2 * o2) / (e0 + e1 + e2)
    y_b = dot(comb.astype(bf16), wb_ref[...])

    half = D_MODEL // 2
    merged = []
    for cols, wga, bga, wgb2, bgb2 in ((slice(0, half), wga0_ref, bga0_ref, wgb0_ref, bgb0_ref),
                                       (slice(half, D_MODEL), wga1_ref, bga1_ref, wgb1_ref, bgb1_ref)):
        g_a = jax.nn.sigmoid(dot(lhs, wga[...]) + bga[...])
        g_b = jax.nn.sigmoid(dot(lhs, wgb2[...]) + bgb2[...])
        merged.append((g_a * y_a[:, cols] + g_b * y_b[:, cols]).astype(bf16))
    mix = dot(jnp.concatenate(merged, axis=-1), wo_ref[...]) + bo_ref[...]
    out_ref[...] = _layer_norm(ALPHA * h_ref[...] + mix, g1_ref[...], b1_ref[...])


def _mix(h, hb, outs, stats, w_in, b_in, weights, *, tm):
    bsz, seq, dm = h.shape
    nhalo = tm // BF16_ROWS
    nrow_blk = seq // BF16_ROWS
    half = dm // 2
    tile = pl.BlockSpec((None, tm, dm), lambda b, i: (b, i, 0))
    att = lambda d: pl.BlockSpec((None, d, tm // d, ATT_GROUP_W), lambda b, i: (b, 0, i, 0))
    att_specs = [att(d) for _, d in GROUPS]
    col_blocks = [(OFF_B, dm), (OFF_C, dm), (OFF_H, dm),
                  (OFF_GA, half), (OFF_GA + half, half), (OFF_GB, half), (OFF_GB + half, half)]
    slab = pltpu.VMEM((ATT_GROUP_W // LANES, tm, LANES), jnp.float32)
    return pl.pallas_call(
        functools.partial(_mix_kernel, tm=tm),
        grid=(bsz, seq // tm),
        in_specs=[tile, tile,
                  pl.BlockSpec((None, BF16_ROWS, dm), lambda b, i: (b, jnp.maximum(i * nhalo - 1, 0), 0)),
                  pl.BlockSpec((None, BF16_ROWS, dm),
                               lambda b, i: (b, jnp.minimum((i + 1) * nhalo, nrow_blk - 1), 0))]
                 + att_specs + att_specs
                 + [_resident_cols(w_in, off, width) for off, width in col_blocks]
                 + [_resident_cols(b_in, off, width) for off, width in col_blocks]
                 + [_resident(a) for a in weights],
        out_specs=tile,
        out_shape=jax.ShapeDtypeStruct((bsz, seq, dm), jnp.float32),
        scratch_shapes=[slab, slab, slab, slab],
        compiler_params=pltpu.CompilerParams(
            dimension_semantics=("parallel", "parallel"),
            vmem_limit_bytes=VMEM_LIMIT),
        name="mix",
    )(h, hb, hb, hb, *outs, *stats, *([w_in] * len(col_blocks)), *([b_in] * len(col_blocks)),
      *weights)


def _ffn_kernel(h_ref, hp_ref, hn_ref, wu_ref, bu_ref, cw_ref, cb_ref, wd_ref, bd_ref,
                g_ref, b_ref, out_ref, f_ref, *, tm, chunks):
    i = pl.program_id(1)
    last = pl.num_programs(1) - 1
    f32 = jnp.float32
    h = h_ref[...]
    lhs_all = jnp.concatenate([h, hp_ref[...], hn_ref[...]], axis=0).astype(jnp.bfloat16)
    lhs = lhs_all[0:tm, :]
    lo = 0
    for ck in chunks:
        cols = slice(lo, lo + ck)
        gcols = slice(D_FF + lo, D_FF + lo + ck)
        a_all = jnp.dot(lhs_all, wu_ref[:, cols], preferred_element_type=f32) + bu_ref[:, cols]
        gate = jnp.dot(lhs, wu_ref[:, gcols], preferred_element_type=f32) + bu_ref[:, gcols]
        a = a_all[0:tm, :]
        a_prev = jnp.where(i > 0, a_all[tm + F32_ROWS - 1:tm + F32_ROWS, :], 0.0)
        a_next = jnp.where(i < last, a_all[tm + F32_ROWS:tm + F32_ROWS + 1, :], 0.0)
        pre = _dwconv3_rows(a, a_prev, a_next, cw_ref.at[:, cols]) + cb_ref[:, cols]
        f = 0.5 * pre * (1.0 + lax.erf(pre * (2.0 ** -0.5))) * gate
        f_ref[:, cols] = f.astype(jnp.bfloat16)
        lo += ck
    ffn = jnp.dot(f_ref[...], wd_ref[...], preferred_element_type=f32) + bd_ref[...]
    out_ref[...] = _layer_norm(ALPHA * h + ffn, g_ref[...], b_ref[...])


def _ffn(h1, wu, bu, cw, cb, wd, bd, g, b, *, tm, chunks):
    bsz, seq, dm = h1.shape
    assert sum(chunks) == D_FF
    nhalo = tm // F32_ROWS
    nrow_blk = seq // F32_ROWS
    weights = (wu, bu, cw, cb, wd, bd, g, b)
    return pl.pallas_call(
        functools.partial(_ffn_kernel, tm=tm, chunks=chunks),
        grid=(bsz, seq // tm),
        in_specs=[
            pl.BlockSpec((None, tm, dm), lambda bb, i: (bb, i, 0)),
            pl.BlockSpec((None, F32_ROWS, dm), lambda bb, i: (bb, jnp.maximum(i * nhalo - 1, 0), 0)),
            pl.BlockSpec((None, F32_ROWS, dm),
                         lambda bb, i: (bb, jnp.minimum((i + 1) * nhalo, nrow_blk - 1), 0)),
        ] + [_resident(a) for a in weights],
        out_specs=pl.BlockSpec((None, tm, dm), lambda bb, i: (bb, i, 0)),
        out_shape=jax.ShapeDtypeStruct((bsz, seq, dm), jnp.float32),
        scratch_shapes=[pltpu.VMEM((tm, D_FF), jnp.bfloat16)],
        compiler_params=pltpu.CompilerParams(
            dimension_semantics=("parallel", "parallel"),
            vmem_limit_bytes=VMEM_LIMIT),
        name="ffn",
    )(h1, h1, h1, *weights)


def kernel(x, ln0_g, ln0_b, w_in, b_in, conv_w, w_a, w_b, w_o, b_o, ln1_g, ln1_b,
           w_up, b_up, ffn_conv_w, ffn_conv_b, w_down, b_down, ln2_g, ln2_b):
    bf16 = jnp.bfloat16
    row = lambda v: v.reshape(1, -1)
    slopes = _alibi_slopes()
    g0, b0 = row(ln0_g), row(ln0_b)
    bsz, seq, _ = x.shape
    l = 0
    wi, bi = w_in[l].astype(bf16), row(b_in[l])
    *qkvs, h, hb = _qkv_proj(x, g0, b0, wi, bi, tm=512)
    outs, stats = [], []
    for g, (_, dil) in enumerate(GROUPS):
        o, st = _attention(qkvs[g], dil=dil, slopes=slopes[g], tq=min(1024, seq // dil), name=f"attn{g}")
        outs.append(o)
        stats.append(st)
    mix_weights = (conv_w[l], w_a[l].astype(bf16), w_b[l].astype(bf16), w_o[l].astype(bf16),
                   row(b_o[l]), row(ln1_g[l]), row(ln1_b[l]))
    h1 = _mix(h, hb, outs, stats, wi, bi, mix_weights, tm=512)
    return _ffn(h1, w_up[l].astype(bf16), row(b_up[l]), ffn_conv_w[l], row(ffn_conv_b[l]),
                w_down[l].astype(bf16), row(b_down[l]), row(ln2_g[l]), row(ln2_b[l]),
                tm=1024, chunks=_chunks(D_FF, 2 * MXU_TILE))
```

```python
import functools

import numpy as np
import jax
import jax.numpy as jnp
from jax import lax
from jax.experimental import pallas as pl
from jax.experimental.pallas import tpu as pltpu

D_MODEL = 1024
D_CONV = D_MODEL
HEAD_DIM = 64
HEADS_PER_GROUP = 8
GROUPS = ((128, 1), (512, 4), (2048, 16))
N_GROUPS = len(GROUPS)
REGROUPED = tuple(g for g, (_, d) in enumerate(GROUPS) if d > 1)
N_ATT_HEADS = N_GROUPS * HEADS_PER_GROUP
ATT_GROUP_W = HEADS_PER_GROUP * HEAD_DIM
ATT_QKV_W = N_GROUPS * ATT_GROUP_W
D_FF = 2816
LN_EPS = 1e-5
DEPTH = 1
assert DEPTH == 1
ALPHA = (2.0 * DEPTH) ** 0.25

OFF_B = 0
OFF_C = OFF_B + D_CONV
OFF_H = OFF_C + D_CONV
OFF_Q = OFF_H + D_CONV
OFF_K = OFF_Q + ATT_QKV_W
OFF_V = OFF_K + ATT_QKV_W
OFF_GA = OFF_V + ATT_QKV_W
OFF_GB = OFF_GA + D_MODEL

RADIUS = 64
QBLK = 2 * RADIUS
KBLK = QBLK + 2 * RADIUS
LANES = 128
BF16_ROWS = 16
F32_ROWS = 8
HEADS_PER_DOT = 4
DOT_W = HEADS_PER_DOT * HEAD_DIM
MXU_TILE = 256
ATTN_ROWS_PER_STEP = 1024
VMEM_LIMIT = 60000 * 1024

assert all(w // (2 * d) == RADIUS for w, d in GROUPS)
assert D_FF % MXU_TILE == 0


def _chunks(total, width):
    full = [width] * (total // width)
    rem = total - sum(full)
    return tuple(full + ([rem] if rem else []))


def _layer_norm(xf, g, b):
    mu = jnp.mean(xf, -1, keepdims=True)
    xc = xf - mu
    var = jnp.mean(xc * xc, -1, keepdims=True)
    return xc * lax.rsqrt(var + LN_EPS) * g + b


def _alibi_slopes():
    n = N_ATT_HEADS
    return np.exp2(-8.0 * np.arange(1, n + 1, dtype=np.float64) / n).reshape(N_GROUPS, HEADS_PER_GROUP)


def _resident(a):
    return pl.BlockSpec(a.shape, lambda b, i: (0,) * a.ndim, pipeline_mode=pl.Buffered(1))


def _resident_cols(a, offset, width):
    assert offset % width == 0
    return pl.BlockSpec((a.shape[0], width), lambda b, i: (0, offset // width),
                        pipeline_mode=pl.Buffered(1))


def _qkv_proj_kernel(x_ref, g_ref, b_ref, *refs, tm, sub):
    n_w = 3 * N_GROUPS
    w_refs, bias_refs = refs[:n_w], refs[n_w:2 * n_w]
    o_refs = refs[2 * n_w:2 * n_w + N_GROUPS]
    h_ref, hb_out_ref, hb_ref, slab_ref = refs[2 * n_w + N_GROUPS:]
    n_slabs = D_MODEL // LANES
    for s in range(tm // sub):
        rows = slice(s * sub, (s + 1) * sub)
        h = _layer_norm(x_ref[rows, :], g_ref[...], b_ref[...])
        hb = h.astype(jnp.bfloat16)
        h_ref[rows, :] = h
        hb_out_ref[rows, :] = hb
        for c in range(n_slabs):
            slab_ref[s, c] = h[:, c * LANES:(c + 1) * LANES]
        for g, (_, dil) in enumerate(GROUPS):
            n = sub // dil
            if dil == 1:
                lhs = hb
            else:
                slot = REGROUPED.index(g)
                for r in range(dil):
                    for c in range(n_slabs):
                        hb_ref[s, slot, r * n:(r + 1) * n, c * LANES:(c + 1) * LANES] = (
                            slab_ref[s, c, pl.ds(r, n, stride=dil), :].astype(jnp.bfloat16))
                lhs = hb_ref[s, slot]
            for t in range(3):
                acc = (jnp.dot(lhs, w_refs[3 * g + t][...], preferred_element_type=jnp.float32)
                       + bias_refs[3 * g + t][...])
                if t == 0:
                    acc = acc * HEAD_DIM ** -0.5
                cols = slice(t * ATT_GROUP_W, (t + 1) * ATT_GROUP_W)
                for r in range(dil):
                    o_refs[g][r, s * n:(s + 1) * n, cols] = (
                        acc[r * n:(r + 1) * n, :].astype(o_refs[g].dtype))


def _qkv_proj(x, ln_g, ln_b, w_in, b_in, *, tm, sub):
    bsz, seq, dm = x.shape
    n = 3 * ATT_GROUP_W
    offsets = [off + g * ATT_GROUP_W for g in range(N_GROUPS) for off in (OFF_Q, OFF_K, OFF_V)]
    tile = pl.BlockSpec((None, tm, dm), lambda b, i: (b, i, 0))
    return pl.pallas_call(
        functools.partial(_qkv_proj_kernel, tm=tm, sub=sub),
        grid=(bsz, seq // tm),
        in_specs=[tile, _resident(ln_g), _resident(ln_b)]
                 + [_resident_cols(w_in, off, ATT_GROUP_W) for off in offsets]
                 + [_resident_cols(b_in, off, ATT_GROUP_W) for off in offsets],
        out_specs=[pl.BlockSpec((None, d, tm // d, n), lambda b, i: (b, 0, i, 0)) for _, d in GROUPS]
                  + [tile, tile],
        out_shape=[jax.ShapeDtypeStruct((bsz, d, seq // d, n), jnp.bfloat16) for _, d in GROUPS]
                  + [jax.ShapeDtypeStruct((bsz, seq, dm), jnp.float32),
                     jax.ShapeDtypeStruct((bsz, seq, dm), jnp.bfloat16)],
        scratch_shapes=[pltpu.VMEM((tm // sub, len(REGROUPED), sub, dm), jnp.bfloat16),
                        pltpu.VMEM((tm // sub, dm // LANES, sub, LANES), jnp.float32)],
        compiler_params=pltpu.CompilerParams(
            dimension_semantics=("parallel", "parallel"),
            vmem_limit_bytes=VMEM_LIMIT),
        name="proj_qkv",
    )(x, ln_g, ln_b, *([w_in] * len(offsets)), *([b_in] * len(offsets)))


def _attn_kernel(q_ref, kp_ref, kc_ref, kn_ref, vp_ref, vc_ref, vn_ref, o_ref, st_ref,
                 bias_ref, *, dil, slopes, sub_len, tq, nres):
    i = pl.program_id(2)
    nq = tq // QBLK

    def window(prev_ref, cur_ref, next_ref, rr, j, lanes):
        lo, hi = j * QBLK - RADIUS, (j + 1) * QBLK + RADIUS
        parts = []
        if lo < 0:
            parts.append(prev_ref[rr, :, lanes])
        parts.append(cur_ref[rr, max(lo, 0):min(hi, tq), lanes])
        if hi > tq:
            parts.append(next_ref[rr, :, lanes])
        return parts[0] if len(parts) == 1 else jnp.concatenate(parts, axis=0)

    @pl.when((pl.program_id(0) == 0) & (pl.program_id(1) == 0) & (i == 0))
    def _():
        row = lax.broadcasted_iota(jnp.int32, (QBLK, KBLK), 0)
        col = lax.broadcasted_iota(jnp.int32, (QBLK, KBLK), 1)
        rel = jnp.abs(col - RADIUS - row)
        band = rel <= RADIUS
        dist = (rel * dil).astype(jnp.float32)
        for var, ok in enumerate((band, band & (col >= RADIUS), band & (col < KBLK - RADIUS))):
            for h in range(HEADS_PER_GROUP):
                bias_ref[var, h] = jnp.where(ok, -slopes[h] * dist, -jnp.inf)

    head_of_lane = lax.broadcasted_iota(jnp.int32, (1, DOT_W), 1) // HEAD_DIM
    f32 = jnp.float32
    last_blk = sub_len // QBLK - 1

    for rr, j in [(rr, j) for rr in range(nres) for j in range(nq)]:
        rows = slice(j * QBLK, (j + 1) * QBLK)
        blk = i * nq + j
        var = jnp.where(blk == 0, 1, jnp.where(blk == last_blk, 2, 0))
        for part in range(HEADS_PER_GROUP // HEADS_PER_DOT):
            lanes = slice(part * DOT_W, (part + 1) * DOT_W)
            qb = q_ref[rr, rows, lanes]
            kb = window(kp_ref, kc_ref, kn_ref, rr, j, lanes)
            vb = window(vp_ref, vc_ref, vn_ref, rr, j, lanes)
            q_heads = jnp.concatenate(
                [jnp.where(head_of_lane == hh, qb, jnp.zeros_like(qb)) for hh in range(HEADS_PER_DOT)],
                axis=0)
            s_all = lax.dot_general(q_heads, kb, (((1,), (1,)), ((), ())), preferred_element_type=f32)
            p_heads, inv_den, lse = [], [], []
            for hh in range(HEADS_PER_DOT):
                s = s_all[hh * QBLK:(hh + 1) * QBLK, :] + bias_ref[var, part * HEADS_PER_DOT + hh]
                m = jnp.max(s, -1, keepdims=True)
                p = jnp.exp(s - m)
                den = jnp.sum(p, -1, keepdims=True)
                p_heads.append(p.astype(jnp.bfloat16))
                inv_den.append(1.0 / den)
                lse.append(m + jnp.log(den))
            o = jnp.zeros((QBLK, DOT_W), f32)
            st = jnp.zeros((QBLK, DOT_W), f32)
            for hh in range(HEADS_PER_DOT):
                sel = head_of_lane == hh
                o_hh = jnp.dot(p_heads[hh], vb, preferred_element_type=f32)
                o = jnp.where(sel, o_hh * inv_den[hh], o)
                st = jnp.where(sel, lse[hh], st)
            o_ref[rr, rows, lanes] = o.astype(o_ref.dtype)
            st_ref[rr, rows, lanes] = st


def _attention(qkv, *, dil, slopes, tq, nres, name):
    bsz, _, sub, _ = qkv.shape
    w = ATT_GROUP_W
    nq = tq // RADIUS
    nblk = sub // RADIUS
    cur = lambda c: pl.BlockSpec((None, nres, tq, w), lambda b, r, i: (b, r, i, c))
    prev = lambda c: pl.BlockSpec((None, nres, RADIUS, w),
                                  lambda b, r, i: (b, r, jnp.maximum(i * nq - 1, 0), c))
    nxt = lambda c: pl.BlockSpec((None, nres, RADIUS, w),
                                 lambda b, r, i: (b, r, jnp.minimum((i + 1) * nq, nblk - 1), c))
    out_spec = pl.BlockSpec((None, nres, tq, w), lambda b, r, i: (b, r, i, 0))
    return pl.pallas_call(
        functools.partial(_attn_kernel, dil=dil, slopes=tuple(float(s) for s in slopes),
                          sub_len=sub, tq=tq, nres=nres),
        grid=(bsz, dil // nres, sub // tq),
        in_specs=[cur(0), prev(1), cur(1), nxt(1), prev(2), cur(2), nxt(2)],
        out_specs=[out_spec, out_spec],
        out_shape=[jax.ShapeDtypeStruct((bsz, dil, sub, w), jnp.bfloat16),
                   jax.ShapeDtypeStruct((bsz, dil, sub, w), jnp.float32)],
        scratch_shapes=[pltpu.VMEM((3, HEADS_PER_GROUP, QBLK, KBLK), jnp.float32)],
        compiler_params=pltpu.CompilerParams(
            dimension_semantics=("arbitrary", "arbitrary", "arbitrary"),
            vmem_limit_bytes=VMEM_LIMIT),
        name=name,
    )(qkv, qkv, qkv, qkv, qkv, qkv, qkv)


def _dwconv3_rows(u, prev_row, next_row, w_ref):
    n = u.shape[0]
    row = lax.broadcasted_iota(jnp.int32, u.shape, 0)
    u_dn = jnp.where(row == 0, prev_row, pltpu.roll(u, 1, axis=0))
    u_up = jnp.where(row == n - 1, next_row, pltpu.roll(u, n - 1, axis=0))
    return u_dn * w_ref[0:1, :] + u * w_ref[1:2, :] + u_up * w_ref[2:3, :]


def _to_natural_order(src_ref, slab_ref):
    dil, n, w = src_ref.shape
    for r in range(dil):
        blk = src_ref[r].astype(jnp.float32)
        for c in range(w // LANES):
            slab_ref[c, pl.ds(r, n, stride=dil), :] = blk[:, c * LANES:(c + 1) * LANES]
    return jnp.concatenate([slab_ref[c] for c in range(w // LANES)], axis=-1)


def _mix_kernel(h_ref, hb_ref, hbp_ref, hbn_ref,
                o0_ref, o1_ref, o2_ref, s0_ref, s1_ref, s2_ref,
                wgb_ref, wc_ref, wh_ref, wga0_ref, wga1_ref, wgb0_ref, wgb1_ref,
                bgb_ref, bc_ref, bh_ref, bga0_ref, bga1_ref, bgb0_ref, bgb1_ref,
                cw_ref, wa_ref, wb_ref, wo_ref, bo_ref, g1_ref, b1_ref, out_ref,
                slab_o1, slab_s1, slab_o2, slab_s2, *, tm):
    i = pl.program_id(1)
    last = pl.num_programs(1) - 1
    f32, bf16 = jnp.float32, jnp.bfloat16
    lhs = hb_ref[...]
    lhs_all = jnp.concatenate([lhs, hbp_ref[...], hbn_ref[...]], axis=0)
    dot = functools.partial(jnp.dot, preferred_element_type=f32)

    u_all = (dot(lhs_all, wc_ref[...]) + bc_ref[...]) * (dot(lhs_all, wh_ref[...]) + bh_ref[...])
    u = u_all[0:tm, :]
    u_prev = jnp.where(i > 0, u_all[tm + BF16_ROWS - 1:tm + BF16_ROWS, :], 0.0)
    u_next = jnp.where(i < last, u_all[tm + BF16_ROWS:tm + BF16_ROWS + 1, :], 0.0)
    conv = _dwconv3_rows(u, u_prev, u_next, cw_ref)
    gate_b = dot(lhs, wgb_ref[...]) + bgb_ref[...]
    y_a = dot((gate_b * conv).astype(bf16), wa_ref[...])

    o0, s0 = o0_ref[0].astype(f32), s0_ref[0]
    o1, s1 = _to_natural_order(o1_ref, slab_o1), _to_natural_order(s1_ref, slab_s1)
    o2, s2 = _to_natural_order(o2_ref, slab_o2), _to_natural_order(s2_ref, slab_s2)
    m = jnp.maximum(jnp.maximum(s0, s1), s2)
    e0, e1, e2 = jnp.exp(s0 - m), jnp.exp(s1 - m), jnp.exp(s2 - m)
    comb = (e0 * o0 + e1 * o1 + e2 * o2) / (e0 + e1 + e2)
    y_b = dot(comb.astype(bf16), wb_ref[...])

    half = D_MODEL // 2
    merged = []
    for cols, wga, bga, wgb2, bgb2 in ((slice(0, half), wga0_ref, bga0_ref, wgb0_ref, bgb0_ref),
                                       (slice(half, D_MODEL), wga1_ref, bga1_ref, wgb1_ref, bgb1_ref)):
        g_a = jax.nn.sigmoid(dot(lhs, wga[...]) + bga[...])
        g_b = jax.nn.sigmoid(dot(lhs, wgb2[...]) + bgb2[...])
        merged.append((g_a * y_a[:, cols] + g_b * y_b[:, cols]).astype(bf16))
    mix = dot(jnp.concatenate(merged, axis=-1), wo_ref[...]) + bo_ref[...]
    out_ref[...] = _layer_norm(ALPHA * h_ref[...] + mix, g1_ref[...], b1_ref[...])


def _mix(h, hb, outs, stats, w_in, b_in, weights, *, tm):
    bsz, seq, dm = h.shape
    nhalo = tm // BF16_ROWS
    nrow_blk = seq // BF16_ROWS
    half = dm // 2
    tile = pl.BlockSpec((None, tm, dm), lambda b, i: (b, i, 0))
    att = lambda d: pl.BlockSpec((None, d, tm // d, ATT_GROUP_W), lambda b, i: (b, 0, i, 0))
    att_specs = [att(d) for _, d in GROUPS]
    col_blocks = [(OFF_B, dm), (OFF_C, dm), (OFF_H, dm),
                  (OFF_GA, half), (OFF_GA + half, half), (OFF_GB, half), (OFF_GB + half, half)]
    slab = pltpu.VMEM((ATT_GROUP_W // LANES, tm, LANES), jnp.float32)
    return pl.pallas_call(
        functools.partial(_mix_kernel, tm=tm),
        grid=(bsz, seq // tm),
        in_specs=[tile, tile,
                  pl.BlockSpec((None, BF16_ROWS, dm), lambda b, i: (b, jnp.maximum(i * nhalo - 1, 0), 0)),
                  pl.BlockSpec((None, BF16_ROWS, dm),
                               lambda b, i: (b, jnp.minimum((i + 1) * nhalo, nrow_blk - 1), 0))]
                 + att_specs + att_specs
                 + [_resident_cols(w_in, off, width) for off, width in col_blocks]
                 + [_resident_cols(b_in, off, width) for off, width in col_blocks]
                 + [_resident(a) for a in weights],
        out_specs=tile,
        out_shape=jax.ShapeDtypeStruct((bsz, seq, dm), jnp.float32),
        scratch_shapes=[slab, slab, slab, slab],
        compiler_params=pltpu.CompilerParams(
            dimension_semantics=("parallel", "parallel"),
            vmem_limit_bytes=VMEM_LIMIT),
        name="mix",
    )(h, hb, hb, hb, *outs, *stats, *([w_in] * len(col_blocks)), *([b_in] * len(col_blocks)),
      *weights)


def _ffn_kernel(h_ref, hp_ref, hn_ref, wu_ref, bu_ref, cw_ref, cb_ref, wd_ref, bd_ref,
                g_ref, b_ref, out_ref, f_ref, *, tm, chunks):
    i = pl.program_id(1)
    last = pl.num_programs(1) - 1
    f32 = jnp.float32
    h = h_ref[...]
    lhs_all = jnp.concatenate([h, hp_ref[...], hn_ref[...]], axis=0).astype(jnp.bfloat16)
    lhs = lhs_all[0:tm, :]
    lo = 0
    for ck in chunks:
        cols = slice(lo, lo + ck)
        gcols = slice(D_FF + lo, D_FF + lo + ck)
        a_all = jnp.dot(lhs_all, wu_ref[:, cols], preferred_element_type=f32) + bu_ref[:, cols]
        gate = jnp.dot(lhs, wu_ref[:, gcols], preferred_element_type=f32) + bu_ref[:, gcols]
        a = a_all[0:tm, :]
        a_prev = jnp.where(i > 0, a_all[tm + F32_ROWS - 1:tm + F32_ROWS, :], 0.0)
        a_next = jnp.where(i < last, a_all[tm + F32_ROWS:tm + F32_ROWS + 1, :], 0.0)
        pre = _dwconv3_rows(a, a_prev, a_next, cw_ref.at[:, cols]) + cb_ref[:, cols]
        f = 0.5 * pre * (1.0 + lax.erf(pre * (2.0 ** -0.5))) * gate
        f_ref[:, cols] = f.astype(jnp.bfloat16)
        lo += ck
    ffn = jnp.dot(f_ref[...], wd_ref[...], preferred_element_type=f32) + bd_ref[...]
    out_ref[...] = _layer_norm(ALPHA * h + ffn, g_ref[...], b_ref[...])


def _ffn(h1, wu, bu, cw, cb, wd, bd, g, b, *, tm, chunks):
    bsz, seq, dm = h1.shape
    assert sum(chunks) == D_FF
    nhalo = tm // F32_ROWS
    nrow_blk = seq // F32_ROWS
    weights = (wu, bu, cw, cb, wd, bd, g, b)
    return pl.pallas_call(
        functools.partial(_ffn_kernel, tm=tm, chunks=chunks),
        grid=(bsz, seq // tm),
        in_specs=[
            pl.BlockSpec((None, tm, dm), lambda bb, i: (bb, i, 0)),
            pl.BlockSpec((None, F32_ROWS, dm), lambda bb, i: (bb, jnp.maximum(i * nhalo - 1, 0), 0)),
            pl.BlockSpec((None, F32_ROWS, dm),
                         lambda bb, i: (bb, jnp.minimum((i + 1) * nhalo, nrow_blk - 1), 0)),
        ] + [_resident(a) for a in weights],
        out_specs=pl.BlockSpec((None, tm, dm), lambda bb, i: (bb, i, 0)),
        out_shape=jax.ShapeDtypeStruct((bsz, seq, dm), jnp.float32),
        scratch_shapes=[pltpu.VMEM((tm, D_FF), jnp.bfloat16)],
        compiler_params=pltpu.CompilerParams(
            dimension_semantics=("parallel", "parallel"),
            vmem_limit_bytes=VMEM_LIMIT),
        name="ffn",
    )(h1, h1, h1, *weights)


def kernel(x, ln0_g, ln0_b, w_in, b_in, conv_w, w_a, w_b, w_o, b_o, ln1_g, ln1_b,
           w_up, b_up, ffn_conv_w, ffn_conv_b, w_down, b_down, ln2_g, ln2_b):
    bf16 = jnp.bfloat16
    row = lambda v: v.reshape(1, -1)
    slopes = _alibi_slopes()
    g0, b0 = row(ln0_g), row(ln0_b)
    bsz, seq, _ = x.shape
    l = 0
    wi, bi = w_in[l].astype(bf16), row(b_in[l])
    *qkvs, h, hb = _qkv_proj(x, g0, b0, wi, bi, tm=1024, sub=512)
    outs, stats = [], []
    for g, (_, dil) in enumerate(GROUPS):
        tq = min(ATTN_ROWS_PER_STEP, seq // dil)
        o, st = _attention(qkvs[g], dil=dil, slopes=slopes[g], tq=tq,
                           nres=ATTN_ROWS_PER_STEP // tq, name=f"attn{g}")
        outs.append(o)
        stats.append(st)
    mix_weights = (conv_w[l], w_a[l].astype(bf16), w_b[l].astype(bf16), w_o[l].astype(bf16),
                   row(b_o[l]), row(ln1_g[l]), row(ln1_b[l]))
    h1 = _mix(h, hb, outs, stats, wi, bi, mix_weights, tm=512)
    return _ffn(h1, w_up[l].astype(bf16), row(b_up[l]), ffn_conv_w[l], row(ffn_conv_b[l]),
                w_down[l].astype(bf16), row(b_down[l]), row(ln2_g[l]), row(ln2_b[l]),
                tm=1024, chunks=_chunks(D_FF, 2 * MXU_TILE))
```

```python
import functools

import numpy as np
import jax
import jax.numpy as jnp
from jax import lax
from jax.experimental import pallas as pl
from jax.experimental.pallas import tpu as pltpu

D_MODEL = 1024
D_CONV = D_MODEL
HEAD_DIM = 64
HEADS_PER_GROUP = 8
GROUPS = ((128, 1), (512, 4), (2048, 16))
N_GROUPS = len(GROUPS)
REGROUPED = tuple(g for g, (_, d) in enumerate(GROUPS) if d > 1)
N_ATT_HEADS = N_GROUPS * HEADS_PER_GROUP
ATT_GROUP_W = HEADS_PER_GROUP * HEAD_DIM
ATT_QKV_W = N_GROUPS * ATT_GROUP_W
D_FF = 2816
LN_EPS = 1e-5
DEPTH = 1
assert DEPTH == 1
ALPHA = (2.0 * DEPTH) ** 0.25

OFF_B = 0
OFF_C = OFF_B + D_CONV
OFF_H = OFF_C + D_CONV
OFF_Q = OFF_H + D_CONV
OFF_K = OFF_Q + ATT_QKV_W
OFF_V = OFF_K + ATT_QKV_W
OFF_GA = OFF_V + ATT_QKV_W
OFF_GB = OFF_GA + D_MODEL

RADIUS = 64
QBLK = 2 * RADIUS
KBLK = QBLK + 2 * RADIUS
LANES = 128
BF16_ROWS = 16
F32_ROWS = 8
HEADS_PER_DOT = 4
DOT_W = HEADS_PER_DOT * HEAD_DIM
MXU_TILE = 256
ATTN_ROWS_PER_STEP = 1024
VMEM_LIMIT = 60000 * 1024

assert all(w // (2 * d) == RADIUS for w, d in GROUPS)
assert D_FF % MXU_TILE == 0


def _chunks(total, width):
    full = [width] * (total // width)
    rem = total - sum(full)
    return tuple(full + ([rem] if rem else []))


def _layer_norm(xf, g, b):
    mu = jnp.mean(xf, -1, keepdims=True)
    xc = xf - mu
    var = jnp.mean(xc * xc, -1, keepdims=True)
    return xc * lax.rsqrt(var + LN_EPS) * g + b


def _alibi_slopes():
    n = N_ATT_HEADS
    return np.exp2(-8.0 * np.arange(1, n + 1, dtype=np.float64) / n).reshape(N_GROUPS, HEADS_PER_GROUP)


def _resident(a):
    return pl.BlockSpec(a.shape, lambda b, i: (0,) * a.ndim, pipeline_mode=pl.Buffered(1))


def _resident_cols(a, offset, width):
    assert offset % width == 0
    return pl.BlockSpec((a.shape[0], width), lambda b, i: (0, offset // width),
                        pipeline_mode=pl.Buffered(1))


def _qkv_proj_kernel(x_ref, g_ref, b_ref, *refs, tm, sub):
    n_w = 3 * N_GROUPS
    w_refs, bias_refs = refs[:n_w], refs[n_w:2 * n_w]
    o_refs = refs[2 * n_w:2 * n_w + N_GROUPS]
    h_ref, hb_out_ref, hb_ref, slab_ref = refs[2 * n_w + N_GROUPS:]
    n_slabs = D_MODEL // LANES
    for s in range(tm // sub):
        rows = slice(s * sub, (s + 1) * sub)
        h = _layer_norm(x_ref[rows, :], g_ref[...], b_ref[...])
        hb = h.astype(jnp.bfloat16)
        h_ref[rows, :] = h
        hb_out_ref[rows, :] = hb
        for c in range(n_slabs):
            slab_ref[s, c] = h[:, c * LANES:(c + 1) * LANES]
        for g, (_, dil) in enumerate(GROUPS):
            n = sub // dil
            if dil == 1:
                lhs = hb
            else:
                slot = REGROUPED.index(g)
                for r in range(dil):
                    for c in range(n_slabs):
                        hb_ref[s, slot, r * n:(r + 1) * n, c * LANES:(c + 1) * LANES] = (
                            slab_ref[s, c, pl.ds(r, n, stride=dil), :].astype(jnp.bfloat16))
                lhs = hb_ref[s, slot]
            for t in range(3):
                acc = (jnp.dot(lhs, w_refs[3 * g + t][...], preferred_element_type=jnp.float32)
                       + bias_refs[3 * g + t][...])
                if t == 0:
                    acc = acc * HEAD_DIM ** -0.5
                cols = slice(t * ATT_GROUP_W, (t + 1) * ATT_GROUP_W)
                for r in range(dil):
                    o_refs[g][r, s * n:(s + 1) * n, cols] = (
                        acc[r * n:(r + 1) * n, :].astype(o_refs[g].dtype))


def _qkv_proj(x, ln_g, ln_b, w_in, b_in, *, tm, sub):
    bsz, seq, dm = x.shape
    n = 3 * ATT_GROUP_W
    offsets = [off + g * ATT_GROUP_W for g in range(N_GROUPS) for off in (OFF_Q, OFF_K, OFF_V)]
    tile = pl.BlockSpec((None, tm, dm), lambda b, i: (b, i, 0))
    return pl.pallas_call(
        functools.partial(_qkv_proj_kernel, tm=tm, sub=sub),
        grid=(bsz, seq // tm),
        in_specs=[tile, _resident(ln_g), _resident(ln_b)]
                 + [_resident_cols(w_in, off, ATT_GROUP_W) for off in offsets]
                 + [_resident_cols(b_in, off, ATT_GROUP_W) for off in offsets],
        out_specs=[pl.BlockSpec((None, d, tm // d, n), lambda b, i: (b, 0, i, 0)) for _, d in GROUPS]
                  + [tile, tile],
        out_shape=[jax.ShapeDtypeStruct((bsz, d, seq // d, n), jnp.bfloat16) for _, d in GROUPS]
                  + [jax.ShapeDtypeStruct((bsz, seq, dm), jnp.float32),
                     jax.ShapeDtypeStruct((bsz, seq, dm), jnp.bfloat16)],
        scratch_shapes=[pltpu.VMEM((tm // sub, len(REGROUPED), sub, dm), jnp.bfloat16),
                        pltpu.VMEM((tm // sub, dm // LANES, sub, LANES), jnp.float32)],
        compiler_params=pltpu.CompilerParams(
            dimension_semantics=("parallel", "parallel"),
            vmem_limit_bytes=VMEM_LIMIT),
        name="proj_qkv",
    )(x, ln_g, ln_b, *([w_in] * len(offsets)), *([b_in] * len(offsets)))


def _attn_kernel(q_ref, kp_ref, kc_ref, kn_ref, vp_ref, vc_ref, vn_ref, o_ref, st_ref,
                 bias_ref, *, dil, slopes, sub_len, tq, nres):
    i = pl.program_id(2)
    nq = tq // QBLK

    def window(prev_ref, cur_ref, next_ref, rr, j, lanes):
        lo, hi = j * QBLK - RADIUS, (j + 1) * QBLK + RADIUS
        parts = []
        if lo < 0:
            parts.append(prev_ref[rr, :, lanes])
        parts.append(cur_ref[rr, max(lo, 0):min(hi, tq), lanes])
        if hi > tq:
            parts.append(next_ref[rr, :, lanes])
        return parts[0] if len(parts) == 1 else jnp.concatenate(parts, axis=0)

    @pl.when((pl.program_id(0) == 0) & (pl.program_id(1) == 0) & (i == 0))
    def _():
        row = lax.broadcasted_iota(jnp.int32, (QBLK, KBLK), 0)
        col = lax.broadcasted_iota(jnp.int32, (QBLK, KBLK), 1)
        rel = jnp.abs(col - RADIUS - row)
        band = rel <= RADIUS
        dist = (rel * dil).astype(jnp.float32)
        for var, ok in enumerate((band, band & (col >= RADIUS), band & (col < KBLK - RADIUS))):
            for h in range(HEADS_PER_GROUP):
                bias_ref[var, h] = jnp.where(ok, -slopes[h] * dist, -jnp.inf)

    head_of_lane = lax.broadcasted_iota(jnp.int32, (1, DOT_W), 1) // HEAD_DIM
    f32 = jnp.float32
    last_blk = sub_len // QBLK - 1

    for rr, j in [(rr, j) for rr in range(nres) for j in range(nq)]:
        rows = slice(j * QBLK, (j + 1) * QBLK)
        blk = i * nq + j
        var = jnp.where(blk == 0, 1, jnp.where(blk == last_blk, 2, 0))
        for part in range(HEADS_PER_GROUP // HEADS_PER_DOT):
            lanes = slice(part * DOT_W, (part + 1) * DOT_W)
            qb = q_ref[rr, rows, lanes]
            kb = window(kp_ref, kc_ref, kn_ref, rr, j, lanes)
            vb = window(vp_ref, vc_ref, vn_ref, rr, j, lanes)
            q_heads = jnp.concatenate(
                [jnp.where(head_of_lane == hh, qb, jnp.zeros_like(qb)) for hh in range(HEADS_PER_DOT)],
                axis=0)
            s_all = lax.dot_general(q_heads, kb, (((1,), (1,)), ((), ())), preferred_element_type=f32)
            p_heads, inv_den, lse = [], [], []
            for hh in range(HEADS_PER_DOT):
                s = s_all[hh * QBLK:(hh + 1) * QBLK, :] + bias_ref[var, part * HEADS_PER_DOT + hh]
                m = jnp.max(s, -1, keepdims=True)
                p = jnp.exp(s - m)
                den = jnp.sum(p, -1, keepdims=True)
                p_heads.append(p.astype(jnp.bfloat16))
                inv_den.append(1.0 / den)
                lse.append(m + jnp.log(den))
            o = jnp.zeros((QBLK, DOT_W), f32)
            st = jnp.zeros((QBLK, DOT_W), f32)
            for hh in range(HEADS_PER_DOT):
                sel = head_of_lane == hh
                o_hh = jnp.dot(p_heads[hh], vb, preferred_element_type=f32)
                o = jnp.where(sel, o_hh * inv_den[hh], o)
                st = jnp.where(sel, lse[hh], st)
            o_ref[rr, rows, lanes] = o.astype(o_ref.dtype)
            st_ref[rr, rows, lanes] = st


def _attention(qkv, *, dil, slopes, tq, nres, name):
    bsz, _, sub, _ = qkv.shape
    w = ATT_GROUP_W
    nq = tq // RADIUS
    nblk = sub // RADIUS
    cur = lambda c: pl.BlockSpec((None, nres, tq, w), lambda b, r, i: (b, r, i, c))
    prev = lambda c: pl.BlockSpec((None, nres, RADIUS, w),
                                  lambda b, r, i: (b, r, jnp.maximum(i * nq - 1, 0), c))
    nxt = lambda c: pl.BlockSpec((None, nres, RADIUS, w),
                                 lambda b, r, i: (b, r, jnp.minimum((i + 1) * nq, nblk - 1), c))
    out_spec = pl.BlockSpec((None, nres, tq, w), lambda b, r, i: (b, r, i, 0))
    return pl.pallas_call(
        functools.partial(_attn_kernel, dil=dil, slopes=tuple(float(s) for s in slopes),
                          sub_len=sub, tq=tq, nres=nres),
        grid=(bsz, dil // nres, sub // tq),
        in_specs=[cur(0), prev(1), cur(1), nxt(1), prev(2), cur(2), nxt(2)],
        out_specs=[out_spec, out_spec],
        out_shape=[jax.ShapeDtypeStruct((bsz, dil, sub, w), jnp.bfloat16),
                   jax.ShapeDtypeStruct((bsz, dil, sub, w), jnp.float32)],
        scratch_shapes=[pltpu.VMEM((3, HEADS_PER_GROUP, QBLK, KBLK), jnp.float32)],
        compiler_params=pltpu.CompilerParams(
            dimension_semantics=("arbitrary", "arbitrary", "arbitrary"),
            vmem_limit_bytes=VMEM_LIMIT),
        name=name,
    )(qkv, qkv, qkv, qkv, qkv, qkv, qkv)


def _dwconv3_rows(u, prev_row, next_row, w_ref):
    n = u.shape[0]
    row = lax.broadcasted_iota(jnp.int32, u.shape, 0)
    u_dn = jnp.where(row == 0, prev_row, pltpu.roll(u, 1, axis=0))
    u_up = jnp.where(row == n - 1, next_row, pltpu.roll(u, n - 1, axis=0))
    return u_dn * w_ref[0:1, :] + u * w_ref[1:2, :] + u_up * w_ref[2:3, :]


def _to_natural_order(src_ref, slab_ref, tmp_ref):
    dil, n, w = src_ref.shape
    n_slabs = w // LANES
    if dil <= 4:
        for r in range(dil):
            blk = src_ref[r].astype(jnp.float32)
            for c in range(n_slabs):
                slab_ref[c, pl.ds(r, n, stride=dil), :] = blk[:, c * LANES:(c + 1) * LANES]
    else:
        inner = 4
        outer = dil // inner
        assert outer <= 4
        plane = n * outer
        for r in range(dil):
            a, b = divmod(r, inner)
            blk = src_ref[r].astype(jnp.float32)
            for c in range(n_slabs):
                tmp_ref[c, pl.ds(b * plane + a, n, stride=outer), :] = blk[:, c * LANES:(c + 1) * LANES]
        for b in range(inner):
            for c in range(n_slabs):
                slab_ref[c, pl.ds(b, plane, stride=inner), :] = tmp_ref[c, b * plane:(b + 1) * plane, :]
    return jnp.concatenate([slab_ref[c] for c in range(n_slabs)], axis=-1)


def _mix_kernel(h_ref, hb_ref, hbp_ref, hbn_ref,
                o0_ref, o1_ref, o2_ref, s0_ref, s1_ref, s2_ref,
                wgb_ref, wc_ref, wh_ref, wga0_ref, wga1_ref, wgb0_ref, wgb1_ref,
                bgb_ref, bc_ref, bh_ref, bga0_ref, bga1_ref, bgb0_ref, bgb1_ref,
                cw_ref, wa_ref, wb_ref, wo_ref, bo_ref, g1_ref, b1_ref, out_ref,
                slab_o1, slab_s1, slab_o2, slab_s2, tmp_o, tmp_s, *, tm):
    i = pl.program_id(1)
    last = pl.num_programs(1) - 1
    f32, bf16 = jnp.float32, jnp.bfloat16
    lhs = hb_ref[...]
    lhs_all = jnp.concatenate([lhs, hbp_ref[...], hbn_ref[...]], axis=0)
    dot = functools.partial(jnp.dot, preferred_element_type=f32)

    gate_c = dot(lhs_all, wc_ref[...]) + bc_ref[...]
    hin = dot(lhs_all, wh_ref[...]) + bh_ref[...]
    gate_b = dot(lhs, wgb_ref[...]) + bgb_ref[...]
    u_all = gate_c * hin
    u = u_all[0:tm, :]
    u_prev = jnp.where(i > 0, u_all[tm + BF16_ROWS - 1:tm + BF16_ROWS, :], 0.0)
    u_next = jnp.where(i < last, u_all[tm + BF16_ROWS:tm + BF16_ROWS + 1, :], 0.0)
    conv = _dwconv3_rows(u, u_prev, u_next, cw_ref)
    y_a = dot((gate_b * conv).astype(bf16), wa_ref[...])

    o0, s0 = o0_ref[0].astype(f32), s0_ref[0]
    o1, s1 = _to_natural_order(o1_ref, slab_o1, tmp_o), _to_natural_order(s1_ref, slab_s1, tmp_s)
    o2, s2 = _to_natural_order(o2_ref, slab_o2, tmp_o), _to_natural_order(s2_ref, slab_s2, tmp_s)
    m = jnp.maximum(jnp.maximum(s0, s1), s2)
    e0, e1, e2 = jnp.exp(s0 - m), jnp.exp(s1 - m), jnp.exp(s2 - m)
    comb = (e0 * o0 + e1 * o1 + e2 * o2) / (e0 + e1 + e2)
    y_b = dot(comb.astype(bf16), wb_ref[...])

    half = D_MODEL // 2
    merged = []
    for cols, wga, bga, wgb2, bgb2 in ((slice(0, half), wga0_ref, bga0_ref, wgb0_ref, bgb0_ref),
                                       (slice(half, D_MODEL), wga1_ref, bga1_ref, wgb1_ref, bgb1_ref)):
        g_a = jax.nn.sigmoid(dot(lhs, wga[...]) + bga[...])
        g_b = jax.nn.sigmoid(dot(lhs, wgb2[...]) + bgb2[...])
        merged.append((g_a * y_a[:, cols] + g_b * y_b[:, cols]).astype(bf16))
    merged = jnp.concatenate(merged, axis=-1)
    blocks = [slice(s * (tm // 2), (s + 1) * (tm // 2)) for s in range(2)]
    mix = [dot(merged[rows, :], wo_ref[...]) for rows in blocks]
    for rows, part in zip(blocks, mix):
        out_ref[rows, :] = _layer_norm(ALPHA * h_ref[rows, :] + part + bo_ref[...],
                                       g1_ref[...], b1_ref[...])


def _mix(h, hb, outs, stats, w_in, b_in, weights, *, tm):
    bsz, seq, dm = h.shape
    nhalo = tm // BF16_ROWS
    nrow_blk = seq // BF16_ROWS
    half = dm // 2
    tile = pl.BlockSpec((None, tm, dm), lambda b, i: (b, i, 0))
    att = lambda d: pl.BlockSpec((None, d, tm // d, ATT_GROUP_W), lambda b, i: (b, 0, i, 0))
    att_specs = [att(d) for _, d in GROUPS]
    col_blocks = [(OFF_B, dm), (OFF_C, dm), (OFF_H, dm),
                  (OFF_GA, half), (OFF_GA + half, half), (OFF_GB, half), (OFF_GB + half, half)]
    slab = pltpu.VMEM((ATT_GROUP_W // LANES, tm, LANES), jnp.float32)
    return pl.pallas_call(
        functools.partial(_mix_kernel, tm=tm),
        grid=(bsz, seq // tm),
        in_specs=[tile, tile,
                  pl.BlockSpec((None, BF16_ROWS, dm), lambda b, i: (b, jnp.maximum(i * nhalo - 1, 0), 0)),
                  pl.BlockSpec((None, BF16_ROWS, dm),
                               lambda b, i: (b, jnp.minimum((i + 1) * nhalo, nrow_blk - 1), 0))]
                 + att_specs + att_specs
                 + [_resident_cols(w_in, off, width) for off, width in col_blocks]
                 + [_resident_cols(b_in, off, width) for off, width in col_blocks]
                 + [_resident(a) for a in weights],
        out_specs=tile,
        out_shape=jax.ShapeDtypeStruct((bsz, seq, dm), jnp.float32),
        scratch_shapes=[slab] * 6,
        compiler_params=pltpu.CompilerParams(
            dimension_semantics=("parallel", "parallel"),
            vmem_limit_bytes=VMEM_LIMIT),
        name="mix",
    )(h, hb, hb, hb, *outs, *stats, *([w_in] * len(col_blocks)), *([b_in] * len(col_blocks)),
      *weights)


def _ffn_kernel(h_ref, hp_ref, hn_ref, wu_ref, bu_ref, cw_ref, cb_ref, wd_ref, bd_ref,
                g_ref, b_ref, out_ref, f_ref, *, tm, sub, chunks):
    i = pl.program_id(1)
    last = pl.num_programs(1) - 1
    f32 = jnp.float32
    h = h_ref[...]
    lhs_all = jnp.concatenate([h, hp_ref[...], hn_ref[...]], axis=0).astype(jnp.bfloat16)
    lhs = lhs_all[0:tm, :]
    lo = 0
    for ck in chunks:
        cols = slice(lo, lo + ck)
        gcols = slice(D_FF + lo, D_FF + lo + ck)
        a_all = jnp.dot(lhs_all, wu_ref[:, cols], preferred_element_type=f32) + bu_ref[:, cols]
        gate = jnp.dot(lhs, wu_ref[:, gcols], preferred_element_type=f32) + bu_ref[:, gcols]
        a = a_all[0:tm, :]
        a_prev = jnp.where(i > 0, a_all[tm + F32_ROWS - 1:tm + F32_ROWS, :], 0.0)
        a_next = jnp.where(i < last, a_all[tm + F32_ROWS:tm + F32_ROWS + 1, :], 0.0)
        pre = _dwconv3_rows(a, a_prev, a_next, cw_ref.at[:, cols]) + cb_ref[:, cols]
        f = 0.5 * pre * (1.0 + lax.erf(pre * (2.0 ** -0.5))) * gate
        f_ref[:, cols] = f.astype(jnp.bfloat16)
        lo += ck
    blocks = [slice(s * sub, (s + 1) * sub) for s in range(tm // sub)]
    ffn = [jnp.dot(f_ref[rows, :], wd_ref[...], preferred_element_type=f32) for rows in blocks]
    for rows, part in zip(blocks, ffn):
        out_ref[rows, :] = _layer_norm(ALPHA * h[rows, :] + part + bd_ref[...], g_ref[...], b_ref[...])


def _ffn(h1, wu, bu, cw, cb, wd, bd, g, b, *, tm, sub, chunks):
    bsz, seq, dm = h1.shape
    assert sum(chunks) == D_FF
    nhalo = tm // F32_ROWS
    nrow_blk = seq // F32_ROWS
    weights = (wu, bu, cw, cb, wd, bd, g, b)
    return pl.pallas_call(
        functools.partial(_ffn_kernel, tm=tm, sub=sub, chunks=chunks),
        grid=(bsz, seq // tm),
        in_specs=[
            pl.BlockSpec((None, tm, dm), lambda bb, i: (bb, i, 0)),
            pl.BlockSpec((None, F32_ROWS, dm), lambda bb, i: (bb, jnp.maximum(i * nhalo - 1, 0), 0)),
            pl.BlockSpec((None, F32_ROWS, dm),
                         lambda bb, i: (bb, jnp.minimum((i + 1) * nhalo, nrow_blk - 1), 0)),
        ] + [_resident(a) for a in weights],
        out_specs=pl.BlockSpec((None, tm, dm), lambda bb, i: (bb, i, 0)),
        out_shape=jax.ShapeDtypeStruct((bsz, seq, dm), jnp.float32),
        scratch_shapes=[pltpu.VMEM((tm, D_FF), jnp.bfloat16)],
        compiler_params=pltpu.CompilerParams(
            dimension_semantics=("parallel", "parallel"),
            vmem_limit_bytes=VMEM_LIMIT),
        name="ffn",
    )(h1, h1, h1, *weights)


def kernel(x, ln0_g, ln0_b, w_in, b_in, conv_w, w_a, w_b, w_o, b_o, ln1_g, ln1_b,
           w_up, b_up, ffn_conv_w, ffn_conv_b, w_down, b_down, ln2_g, ln2_b):
    bf16 = jnp.bfloat16
    row = lambda v: v.reshape(1, -1)
    slopes = _alibi_slopes()
    g0, b0 = row(ln0_g), row(ln0_b)
    bsz, seq, _ = x.shape
    l = 0
    wi, bi = w_in[l].astype(bf16), row(b_in[l])
    *qkvs, h, hb = _qkv_proj(x, g0, b0, wi, bi, tm=1024, sub=256)
    outs, stats = [], []
    for g, (_, dil) in enumerate(GROUPS):
        tq = min(ATTN_ROWS_PER_STEP, seq // dil)
        o, st = _attention(qkvs[g], dil=dil, slopes=slopes[g], tq=tq,
                           nres=ATTN_ROWS_PER_STEP // tq, name=f"attn{g}")
        outs.append(o)
        stats.append(st)
    mix_weights = (conv_w[l], w_a[l].astype(bf16), w_b[l].astype(bf16), w_o[l].astype(bf16),
                   row(b_o[l]), row(ln1_g[l]), row(ln1_b[l]))
    h1 = _mix(h, hb, outs, stats, wi, bi, mix_weights, tm=512)
    return _ffn(h1, w_up[l].astype(bf16), row(b_up[l]), ffn_conv_w[l], row(ffn_conv_b[l]),
                w_down[l].astype(bf16), row(b_down[l]), row(ln2_g[l]), row(ln2_b[l]),
                tm=1024, sub=512, chunks=_chunks(D_FF, 2 * MXU_TILE))
```

```python
import functools

import numpy as np
import jax
import jax.numpy as jnp
from jax import lax
from jax.experimental import pallas as pl
from jax.experimental.pallas import tpu as pltpu

D_MODEL = 1024
D_CONV = D_MODEL
HEAD_DIM = 64
HEADS_PER_GROUP = 8
GROUPS = ((128, 1), (512, 4), (2048, 16))
N_GROUPS = len(GROUPS)
REGROUPED = tuple(g for g, (_, d) in enumerate(GROUPS) if d > 1)
N_ATT_HEADS = N_GROUPS * HEADS_PER_GROUP
ATT_GROUP_W = HEADS_PER_GROUP * HEAD_DIM
ATT_QKV_W = N_GROUPS * ATT_GROUP_W
D_FF = 2816
LN_EPS = 1e-5
DEPTH = 1
assert DEPTH == 1
ALPHA = (2.0 * DEPTH) ** 0.25

OFF_B = 0
OFF_C = OFF_B + D_CONV
OFF_H = OFF_C + D_CONV
OFF_Q = OFF_H + D_CONV
OFF_K = OFF_Q + ATT_QKV_W
OFF_V = OFF_K + ATT_QKV_W
OFF_GA = OFF_V + ATT_QKV_W
OFF_GB = OFF_GA + D_MODEL

RADIUS = 64
QBLK = 2 * RADIUS
KBLK = QBLK + 2 * RADIUS
LANES = 128
BF16_ROWS = 16
F32_ROWS = 8
HEADS_PER_DOT = 4
DOT_W = HEADS_PER_DOT * HEAD_DIM
MXU_TILE = 256
ATTN_ROWS_PER_STEP = 2048
VMEM_LIMIT = 60000 * 1024

assert all(w // (2 * d) == RADIUS for w, d in GROUPS)
assert D_FF % MXU_TILE == 0


def _chunks(total, width):
    full = [width] * (total // width)
    rem = total - sum(full)
    return tuple(full + ([rem] if rem else []))


def _layer_norm(xf, g, b):
    mu = jnp.mean(xf, -1, keepdims=True)
    xc = xf - mu
    var = jnp.mean(xc * xc, -1, keepdims=True)
    return xc * lax.rsqrt(var + LN_EPS) * g + b


def _alibi_slopes():
    n = N_ATT_HEADS
    return np.exp2(-8.0 * np.arange(1, n + 1, dtype=np.float64) / n).reshape(N_GROUPS, HEADS_PER_GROUP)


def _resident(a):
    return pl.BlockSpec(a.shape, lambda b, i: (0,) * a.ndim, pipeline_mode=pl.Buffered(1))


def _resident_cols(a, offset, width):
    assert offset % width == 0
    return pl.BlockSpec((a.shape[0], width), lambda b, i: (0, offset // width),
                        pipeline_mode=pl.Buffered(1))


def _qkv_proj_kernel(x_ref, g_ref, b_ref, *refs, tm, sub):
    n_w = 3 * N_GROUPS
    w_refs, bias_refs = refs[:n_w], refs[n_w:2 * n_w]
    o_refs = refs[2 * n_w:2 * n_w + N_GROUPS]
    h_ref, hb_out_ref, hb_ref, slab_ref = refs[2 * n_w + N_GROUPS:]
    n_slabs = D_MODEL // LANES
    for s in range(tm // sub):
        rows = slice(s * sub, (s + 1) * sub)
        h = _layer_norm(x_ref[rows, :], g_ref[...], b_ref[...])
        hb = h.astype(jnp.bfloat16)
        h_ref[rows, :] = h
        hb_out_ref[rows, :] = hb
        for c in range(n_slabs):
            slab_ref[s, c] = h[:, c * LANES:(c + 1) * LANES]
        for g, (_, dil) in enumerate(GROUPS):
            n = sub // dil
            if dil == 1:
                lhs = hb
            else:
                slot = REGROUPED.index(g)
                for r in range(dil):
                    for c in range(n_slabs):
                        hb_ref[s, slot, r * n:(r + 1) * n, c * LANES:(c + 1) * LANES] = (
                            slab_ref[s, c, pl.ds(r, n, stride=dil), :].astype(jnp.bfloat16))
                lhs = hb_ref[s, slot]
            for t in range(3):
                acc = (jnp.dot(lhs, w_refs[3 * g + t][...], preferred_element_type=jnp.float32)
                       + bias_refs[3 * g + t][...])
                if t == 0:
                    acc = acc * HEAD_DIM ** -0.5
                cols = slice(t * ATT_GROUP_W, (t + 1) * ATT_GROUP_W)
                for r in range(dil):
                    o_refs[g][r, s * n:(s + 1) * n, cols] = (
                        acc[r * n:(r + 1) * n, :].astype(o_refs[g].dtype))


def _qkv_proj(x, ln_g, ln_b, w_in, b_in, *, tm, sub):
    bsz, seq, dm = x.shape
    n = 3 * ATT_GROUP_W
    offsets = [off + g * ATT_GROUP_W for g in range(N_GROUPS) for off in (OFF_Q, OFF_K, OFF_V)]
    tile = pl.BlockSpec((None, tm, dm), lambda b, i: (b, i, 0))
    return pl.pallas_call(
        functools.partial(_qkv_proj_kernel, tm=tm, sub=sub),
        grid=(bsz, seq // tm),
        in_specs=[tile, _resident(ln_g), _resident(ln_b)]
                 + [_resident_cols(w_in, off, ATT_GROUP_W) for off in offsets]
                 + [_resident_cols(b_in, off, ATT_GROUP_W) for off in offsets],
        out_specs=[pl.BlockSpec((None, d, tm // d, n), lambda b, i: (b, 0, i, 0)) for _, d in GROUPS]
                  + [tile, tile],
        out_shape=[jax.ShapeDtypeStruct((bsz, d, seq // d, n), jnp.bfloat16) for _, d in GROUPS]
                  + [jax.ShapeDtypeStruct((bsz, seq, dm), jnp.float32),
                     jax.ShapeDtypeStruct((bsz, seq, dm), jnp.bfloat16)],
        scratch_shapes=[pltpu.VMEM((tm // sub, len(REGROUPED), sub, dm), jnp.bfloat16),
                        pltpu.VMEM((tm // sub, dm // LANES, sub, LANES), jnp.float32)],
        compiler_params=pltpu.CompilerParams(
            dimension_semantics=("parallel", "parallel"),
            vmem_limit_bytes=VMEM_LIMIT),
        name="proj_qkv",
    )(x, ln_g, ln_b, *([w_in] * len(offsets)), *([b_in] * len(offsets)))


def _attn_kernel(q_ref, kp_ref, kc_ref, kn_ref, vp_ref, vc_ref, vn_ref, o_ref, st_ref,
                 bias_ref, *, dil, slopes, sub_len, tq, nres):
    i = pl.program_id(2)
    nq = tq // QBLK

    def window(prev_ref, cur_ref, next_ref, rr, j, lanes):
        lo, hi = j * QBLK - RADIUS, (j + 1) * QBLK + RADIUS
        parts = []
        if lo < 0:
            parts.append(prev_ref[rr, :, lanes])
        parts.append(cur_ref[rr, max(lo, 0):min(hi, tq), lanes])
        if hi > tq:
            parts.append(next_ref[rr, :, lanes])
        return parts[0] if len(parts) == 1 else jnp.concatenate(parts, axis=0)

    @pl.when((pl.program_id(0) == 0) & (pl.program_id(1) == 0) & (i == 0))
    def _():
        row = lax.broadcasted_iota(jnp.int32, (QBLK, KBLK), 0)
        col = lax.broadcasted_iota(jnp.int32, (QBLK, KBLK), 1)
        rel = jnp.abs(col - RADIUS - row)
        band = rel <= RADIUS
        dist = (rel * dil).astype(jnp.float32)
        for var, ok in enumerate((band, band & (col >= RADIUS), band & (col < KBLK - RADIUS))):
            for h in range(HEADS_PER_GROUP):
                bias_ref[var, h] = jnp.where(ok, -slopes[h] * dist, -jnp.inf)

    head_of_lane = lax.broadcasted_iota(jnp.int32, (1, DOT_W), 1) // HEAD_DIM
    f32 = jnp.float32
    last_blk = sub_len // QBLK - 1

    for rr, j in [(rr, j) for rr in range(nres) for j in range(nq)]:
        rows = slice(j * QBLK, (j + 1) * QBLK)
        blk = i * nq + j
        var = jnp.where(blk == 0, 1, jnp.where(blk == last_blk, 2, 0))
        for part in range(HEADS_PER_GROUP // HEADS_PER_DOT):
            lanes = slice(part * DOT_W, (part + 1) * DOT_W)
            qb = q_ref[rr, rows, lanes]
            kb = window(kp_ref, kc_ref, kn_ref, rr, j, lanes)
            vb = window(vp_ref, vc_ref, vn_ref, rr, j, lanes)
            q_heads = jnp.concatenate(
                [jnp.where(head_of_lane == hh, qb, jnp.zeros_like(qb)) for hh in range(HEADS_PER_DOT)],
                axis=0)
            s_all = lax.dot_general(q_heads, kb, (((1,), (1,)), ((), ())), preferred_element_type=f32)
            p_heads, inv_den, lse = [], [], []
            for hh in range(HEADS_PER_DOT):
                s = s_all[hh * QBLK:(hh + 1) * QBLK, :] + bias_ref[var, part * HEADS_PER_DOT + hh]
                m = jnp.max(s, -1, keepdims=True)
                p = jnp.exp(s - m)
                den = jnp.sum(p, -1, keepdims=True)
                p_heads.append(p.astype(jnp.bfloat16))
                inv_den.append(1.0 / den)
                lse.append(m + jnp.log(den))
            o = jnp.zeros((QBLK, DOT_W), f32)
            st = jnp.zeros((QBLK, DOT_W), f32)
            for hh in range(HEADS_PER_DOT):
                sel = head_of_lane == hh
                o_hh = jnp.dot(p_heads[hh], vb, preferred_element_type=f32)
                o = jnp.where(sel, o_hh * inv_den[hh], o)
                st = jnp.where(sel, lse[hh], st)
            o_ref[rr, rows, lanes] = o.astype(o_ref.dtype)
            st_ref[rr, rows, lanes] = st


def _attention(qkv, *, dil, slopes, tq, nres, name):
    bsz, _, sub, _ = qkv.shape
    w = ATT_GROUP_W
    nq = tq // RADIUS
    nblk = sub // RADIUS
    cur = lambda c: pl.BlockSpec((None, nres, tq, w), lambda b, r, i: (b, r, i, c))
    prev = lambda c: pl.BlockSpec((None, nres, RADIUS, w),
                                  lambda b, r, i: (b, r, jnp.maximum(i * nq - 1, 0), c))
    nxt = lambda c: pl.BlockSpec((None, nres, RADIUS, w),
                                 lambda b, r, i: (b, r, jnp.minimum((i + 1) * nq, nblk - 1), c))
    out_spec = pl.BlockSpec((None, nres, tq, w), lambda b, r, i: (b, r, i, 0))
    return pl.pallas_call(
        functools.partial(_attn_kernel, dil=dil, slopes=tuple(float(s) for s in slopes),
                          sub_len=sub, tq=tq, nres=nres),
        grid=(bsz, dil // nres, sub // tq),
        in_specs=[cur(0), prev(1), cur(1), nxt(1), prev(2), cur(2), nxt(2)],
        out_specs=[out_spec, out_spec],
        out_shape=[jax.ShapeDtypeStruct((bsz, dil, sub, w), jnp.bfloat16),
                   jax.ShapeDtypeStruct((bsz, dil, sub, w), jnp.float32)],
        scratch_shapes=[pltpu.VMEM((3, HEADS_PER_GROUP, QBLK, KBLK), jnp.float32)],
        compiler_params=pltpu.CompilerParams(
            dimension_semantics=("arbitrary", "arbitrary", "arbitrary"),
            vmem_limit_bytes=VMEM_LIMIT),
        name=name,
    )(qkv, qkv, qkv, qkv, qkv, qkv, qkv)


def _dwconv3_rows(u, prev_row, next_row, w_ref):
    n = u.shape[0]
    row = lax.broadcasted_iota(jnp.int32, u.shape, 0)
    u_dn = jnp.where(row == 0, prev_row, pltpu.roll(u, 1, axis=0))
    u_up = jnp.where(row == n - 1, next_row, pltpu.roll(u, n - 1, axis=0))
    return u_dn * w_ref[0:1, :] + u * w_ref[1:2, :] + u_up * w_ref[2:3, :]


def _to_natural_order(src_ref, slab_ref, tmp_ref):
    dil, n, w = src_ref.shape
    n_slabs = w // LANES
    if dil <= 4:
        for r in range(dil):
            blk = src_ref[r].astype(jnp.float32)
            for c in range(n_slabs):
                slab_ref[c, pl.ds(r, n, stride=dil), :] = blk[:, c * LANES:(c + 1) * LANES]
    else:
        inner = 4
        outer = dil // inner
        assert outer <= 4
        plane = n * outer
        for r in range(dil):
            a, b = divmod(r, inner)
            blk = src_ref[r].astype(jnp.float32)
            for c in range(n_slabs):
                tmp_ref[c, pl.ds(b * plane + a, n, stride=outer), :] = blk[:, c * LANES:(c + 1) * LANES]
        for b in range(inner):
            for c in range(n_slabs):
                slab_ref[c, pl.ds(b, plane, stride=inner), :] = tmp_ref[c, b * plane:(b + 1) * plane, :]
    return jnp.concatenate([slab_ref[c] for c in range(n_slabs)], axis=-1)


def _mix_kernel(h_ref, hb_ref, hbp_ref, hbn_ref,
                o0_ref, o1_ref, o2_ref, s0_ref, s1_ref, s2_ref,
                wgb_ref, wc_ref, wh_ref, wga0_ref, wga1_ref, wgb0_ref, wgb1_ref,
                bgb_ref, bc_ref, bh_ref, bga0_ref, bga1_ref, bgb0_ref, bgb1_ref,
                cw_ref, wa_ref, wb_ref, wo_ref, bo_ref, g1_ref, b1_ref, out_ref,
                slab_o1, slab_s1, slab_o2, slab_s2, tmp_o, tmp_s, *, tm):
    i = pl.program_id(1)
    last = pl.num_programs(1) - 1
    f32, bf16 = jnp.float32, jnp.bfloat16
    lhs = hb_ref[...]
    lhs_all = jnp.concatenate([lhs, hbp_ref[...], hbn_ref[...]], axis=0)
    dot = functools.partial(jnp.dot, preferred_element_type=f32)

    gate_c = dot(lhs_all, wc_ref[...]) + bc_ref[...]
    hin = dot(lhs_all, wh_ref[...]) + bh_ref[...]
    gate_b = dot(lhs, wgb_ref[...]) + bgb_ref[...]
    u_all = gate_c * hin
    u = u_all[0:tm, :]
    u_prev = jnp.where(i > 0, u_all[tm + BF16_ROWS - 1:tm + BF16_ROWS, :], 0.0)
    u_next = jnp.where(i < last, u_all[tm + BF16_ROWS:tm + BF16_ROWS + 1, :], 0.0)
    conv = _dwconv3_rows(u, u_prev, u_next, cw_ref)
    y_a = dot((gate_b * conv).astype(bf16), wa_ref[...])

    o0, s0 = o0_ref[0].astype(f32), s0_ref[0]
    o1, s1 = _to_natural_order(o1_ref, slab_o1, tmp_o), _to_natural_order(s1_ref, slab_s1, tmp_s)
    o2, s2 = _to_natural_order(o2_ref, slab_o2, tmp_o), _to_natural_order(s2_ref, slab_s2, tmp_s)
    m = jnp.maximum(jnp.maximum(s0, s1), s2)
    e0, e1, e2 = jnp.exp(s0 - m), jnp.exp(s1 - m), jnp.exp(s2 - m)
    comb = (e0 * o0 + e1 * o1 + e2 * o2) / (e0 + e1 + e2)
    y_b = dot(comb.astype(bf16), wb_ref[...])

    half = D_MODEL // 2
    merged = []
    for cols, wga, bga, wgb2, bgb2 in ((slice(0, half), wga0_ref, bga0_ref, wgb0_ref, bgb0_ref),
                                       (slice(half, D_MODEL), wga1_ref, bga1_ref, wgb1_ref, bgb1_ref)):
        g_a = jax.nn.sigmoid(dot(lhs, wga[...]) + bga[...])
        g_b = jax.nn.sigmoid(dot(lhs, wgb2[...]) + bgb2[...])
        merged.append((g_a * y_a[:, cols] + g_b * y_b[:, cols]).astype(bf16))
    merged = jnp.concatenate(merged, axis=-1)
    blocks = [slice(s * (tm // 2), (s + 1) * (tm // 2)) for s in range(2)]
    mix = [dot(merged[rows, :], wo_ref[...]) for rows in blocks]
    for rows, part in zip(blocks, mix):
        out_ref[rows, :] = _layer_norm(ALPHA * h_ref[rows, :] + part + bo_ref[...],
                                       g1_ref[...], b1_ref[...])


def _mix(h, hb, outs, stats, w_in, b_in, weights, *, tm):
    bsz, seq, dm = h.shape
    nhalo = tm // BF16_ROWS
    nrow_blk = seq // BF16_ROWS
    half = dm // 2
    tile = pl.BlockSpec((None, tm, dm), lambda b, i: (b, i, 0))
    att = lambda d: pl.BlockSpec((None, d, tm // d, ATT_GROUP_W), lambda b, i: (b, 0, i, 0))
    att_specs = [att(d) for _, d in GROUPS]
    col_blocks = [(OFF_B, dm), (OFF_C, dm), (OFF_H, dm),
                  (OFF_GA, half), (OFF_GA + half, half), (OFF_GB, half), (OFF_GB + half, half)]
    slab = pltpu.VMEM((ATT_GROUP_W // LANES, tm, LANES), jnp.float32)
    return pl.pallas_call(
        functools.partial(_mix_kernel, tm=tm),
        grid=(bsz, seq // tm),
        in_specs=[tile, tile,
                  pl.BlockSpec((None, BF16_ROWS, dm), lambda b, i: (b, jnp.maximum(i * nhalo - 1, 0), 0)),
                  pl.BlockSpec((None, BF16_ROWS, dm),
                               lambda b, i: (b, jnp.minimum((i + 1) * nhalo, nrow_blk - 1), 0))]
                 + att_specs + att_specs
                 + [_resident_cols(w_in, off, width) for off, width in col_blocks]
                 + [_resident_cols(b_in, off, width) for off, width in col_blocks]
                 + [_resident(a) for a in weights],
        out_specs=tile,
        out_shape=jax.ShapeDtypeStruct((bsz, seq, dm), jnp.float32),
        scratch_shapes=[slab] * 6,
        compiler_params=pltpu.CompilerParams(
            dimension_semantics=("parallel", "parallel"),
            vmem_limit_bytes=VMEM_LIMIT),
        name="mix",
    )(h, hb, hb, hb, *outs, *stats, *([w_in] * len(col_blocks)), *([b_in] * len(col_blocks)),
      *weights)


def _ffn_kernel(h_ref, hp_ref, hn_ref, wu_ref, bu_ref, cw_ref, cb_ref, wd_ref, bd_ref,
                g_ref, b_ref, out_ref, f_ref, *, tm, sub, chunks):
    i = pl.program_id(1)
    last = pl.num_programs(1) - 1
    f32 = jnp.float32
    h = h_ref[...]
    lhs_all = jnp.concatenate([h, hp_ref[...], hn_ref[...]], axis=0).astype(jnp.bfloat16)
    lhs = lhs_all[0:tm, :]
    lo = 0
    for ck in chunks:
        cols = slice(lo, lo + ck)
        gcols = slice(D_FF + lo, D_FF + lo + ck)
        a_all = jnp.dot(lhs_all, wu_ref[:, cols], preferred_element_type=f32) + bu_ref[:, cols]
        gate = jnp.dot(lhs, wu_ref[:, gcols], preferred_element_type=f32) + bu_ref[:, gcols]
        a = a_all[0:tm, :]
        a_prev = jnp.where(i > 0, a_all[tm + F32_ROWS - 1:tm + F32_ROWS, :], 0.0)
        a_next = jnp.where(i < last, a_all[tm + F32_ROWS:tm + F32_ROWS + 1, :], 0.0)
        pre = _dwconv3_rows(a, a_prev, a_next, cw_ref.at[:, cols]) + cb_ref[:, cols]
        f = 0.5 * pre * (1.0 + lax.erf(pre * (2.0 ** -0.5))) * gate
        f_ref[:, cols] = f.astype(jnp.bfloat16)
        lo += ck
    blocks = [slice(s * sub, (s + 1) * sub) for s in range(tm // sub)]
    ffn = [jnp.dot(f_ref[rows, :], wd_ref[...], preferred_element_type=f32) for rows in blocks]
    for rows, part in zip(blocks, ffn):
        out_ref[rows, :] = _layer_norm(ALPHA * h[rows, :] + part + bd_ref[...], g_ref[...], b_ref[...])


def _ffn(h1, wu, bu, cw, cb, wd, bd, g, b, *, tm, sub, chunks):
    bsz, seq, dm = h1.shape
    assert sum(chunks) == D_FF
    nhalo = tm // F32_ROWS
    nrow_blk = seq // F32_ROWS
    weights = (wu, bu, cw, cb, wd, bd, g, b)
    return pl.pallas_call(
        functools.partial(_ffn_kernel, tm=tm, sub=sub, chunks=chunks),
        grid=(bsz, seq // tm),
        in_specs=[
            pl.BlockSpec((None, tm, dm), lambda bb, i: (bb, i, 0)),
            pl.BlockSpec((None, F32_ROWS, dm), lambda bb, i: (bb, jnp.maximum(i * nhalo - 1, 0), 0)),
            pl.BlockSpec((None, F32_ROWS, dm),
                         lambda bb, i: (bb, jnp.minimum((i + 1) * nhalo, nrow_blk - 1), 0)),
        ] + [_resident(a) for a in weights],
        out_specs=pl.BlockSpec((None, tm, dm), lambda bb, i: (bb, i, 0)),
        out_shape=jax.ShapeDtypeStruct((bsz, seq, dm), jnp.float32),
        scratch_shapes=[pltpu.VMEM((tm, D_FF), jnp.bfloat16)],
        compiler_params=pltpu.CompilerParams(
            dimension_semantics=("parallel", "parallel"),
            vmem_limit_bytes=VMEM_LIMIT),
        name="ffn",
    )(h1, h1, h1, *weights)


def kernel(x, ln0_g, ln0_b, w_in, b_in, conv_w, w_a, w_b, w_o, b_o, ln1_g, ln1_b,
           w_up, b_up, ffn_conv_w, ffn_conv_b, w_down, b_down, ln2_g, ln2_b):
    bf16 = jnp.bfloat16
    row = lambda v: v.reshape(1, -1)
    slopes = _alibi_slopes()
    g0, b0 = row(ln0_g), row(ln0_b)
    bsz, seq, _ = x.shape
    l = 0
    wi, bi = w_in[l].astype(bf16), row(b_in[l])
    *qkvs, h, hb = _qkv_proj(x, g0, b0, wi, bi, tm=1024, sub=256)
    outs, stats = [], []
    for g, (_, dil) in enumerate(GROUPS):
        tq = min(ATTN_ROWS_PER_STEP, seq // dil)
        o, st = _attention(qkvs[g], dil=dil, slopes=slopes[g], tq=tq,
                           nres=ATTN_ROWS_PER_STEP // tq, name=f"attn{g}")
        outs.append(o)
        stats.append(st)
    mix_weights = (conv_w[l], w_a[l].astype(bf16), w_b[l].astype(bf16), w_o[l].astype(bf16),
                   row(b_o[l]), row(ln1_g[l]), row(ln1_b[l]))
    h1 = _mix(h, hb, outs, stats, wi, bi, mix_weights, tm=512)
    return _ffn(h1, w_up[l].astype(bf16), row(b_up[l]), ffn_conv_w[l], row(ffn_conv_b[l]),
                w_down[l].astype(bf16), row(b_down[l]), row(ln2_g[l]), row(ln2_b[l]),
                tm=1024, sub=512, chunks=_chunks(D_FF, 2 * MXU_TILE))
```

```python
import functools

import numpy as np
import jax
import jax.numpy as jnp
from jax import lax
from jax.experimental import pallas as pl
from jax.experimental.pallas import tpu as pltpu

D_MODEL = 1024
D_CONV = D_MODEL
HEAD_DIM = 64
HEADS_PER_GROUP = 8
GROUPS = ((128, 1), (512, 4), (2048, 16))
N_GROUPS = len(GROUPS)
REGROUPED = tuple(g for g, (_, d) in enumerate(GROUPS) if d > 1)
N_ATT_HEADS = N_GROUPS * HEADS_PER_GROUP
ATT_GROUP_W = HEADS_PER_GROUP * HEAD_DIM
ATT_QKV_W = N_GROUPS * ATT_GROUP_W
D_FF = 2816
LN_EPS = 1e-5
DEPTH = 1
assert DEPTH == 1
ALPHA = (2.0 * DEPTH) ** 0.25

OFF_B = 0
OFF_C = OFF_B + D_CONV
OFF_H = OFF_C + D_CONV
OFF_Q = OFF_H + D_CONV
OFF_K = OFF_Q + ATT_QKV_W
OFF_V = OFF_K + ATT_QKV_W
OFF_GA = OFF_V + ATT_QKV_W
OFF_GB = OFF_GA + D_MODEL

RADIUS = 64
QBLK = 2 * RADIUS
KBLK = QBLK + 2 * RADIUS
LANES = 128
BF16_ROWS = 16
F32_ROWS = 8
HEADS_PER_DOT = 4
DOT_W = HEADS_PER_DOT * HEAD_DIM
MXU_TILE = 256
SIDE_W = 512
ATTN_ROWS_PER_STEP = 1024
VMEM_LIMIT = 60000 * 1024

assert all(w // (2 * d) == RADIUS for w, d in GROUPS)
assert D_FF % MXU_TILE == 0


def _chunks(total, width):
    full = [width] * (total // width)
    rem = total - sum(full)
    return tuple(full + ([rem] if rem else []))


def _layer_norm(xf, g, b):
    mu = jnp.mean(xf, -1, keepdims=True)
    xc = xf - mu
    var = jnp.mean(xc * xc, -1, keepdims=True)
    return xc * lax.rsqrt(var + LN_EPS) * g + b


def _alibi_slopes():
    n = N_ATT_HEADS
    return np.exp2(-8.0 * np.arange(1, n + 1, dtype=np.float64) / n).reshape(N_GROUPS, HEADS_PER_GROUP)


def _resident(a):
    return pl.BlockSpec(a.shape, lambda b, i: (0,) * a.ndim, pipeline_mode=pl.Buffered(1))


def _resident_cols(a, offset, width):
    assert offset % width == 0
    return pl.BlockSpec((a.shape[0], width), lambda b, i: (0, offset // width),
                        pipeline_mode=pl.Buffered(1))


def _qkv_proj_kernel(x_ref, g_ref, b_ref, *refs, tm, sub):
    n_w = 3 * N_GROUPS
    w_refs, bias_refs = refs[:n_w], refs[n_w:2 * n_w]
    o_refs = refs[2 * n_w:2 * n_w + N_GROUPS]
    h_ref, hb_out_ref, hb_ref, slab_ref = refs[2 * n_w + N_GROUPS:]
    n_slabs = D_MODEL // LANES
    for s in range(tm // sub):
        rows = slice(s * sub, (s + 1) * sub)
        h = _layer_norm(x_ref[rows, :], g_ref[...], b_ref[...])
        hb = h.astype(jnp.bfloat16)
        h_ref[rows, :] = h
        hb_out_ref[rows, :] = hb
        for c in range(n_slabs):
            slab_ref[s, c] = h[:, c * LANES:(c + 1) * LANES]
        for g, (_, dil) in enumerate(GROUPS):
            n = sub // dil
            if dil == 1:
                lhs = hb
            else:
                slot = REGROUPED.index(g)
                for r in range(dil):
                    for c in range(n_slabs):
                        hb_ref[s, slot, r * n:(r + 1) * n, c * LANES:(c + 1) * LANES] = (
                            slab_ref[s, c, pl.ds(r, n, stride=dil), :].astype(jnp.bfloat16))
                lhs = hb_ref[s, slot]
            for t in range(3):
                acc = (jnp.dot(lhs, w_refs[3 * g + t][...], preferred_element_type=jnp.float32)
                       + bias_refs[3 * g + t][...])
                if t == 0:
                    acc = acc * HEAD_DIM ** -0.5
                cols = slice(t * ATT_GROUP_W, (t + 1) * ATT_GROUP_W)
                for r in range(dil):
                    o_refs[g][r, s * n:(s + 1) * n, cols] = (
                        acc[r * n:(r + 1) * n, :].astype(o_refs[g].dtype))


def _qkv_proj(x, ln_g, ln_b, w_in, b_in, *, tm, sub):
    bsz, seq, dm = x.shape
    n = 3 * ATT_GROUP_W
    offsets = [off + g * ATT_GROUP_W for g in range(N_GROUPS) for off in (OFF_Q, OFF_K, OFF_V)]
    tile = pl.BlockSpec((None, tm, dm), lambda b, i: (b, i, 0))
    return pl.pallas_call(
        functools.partial(_qkv_proj_kernel, tm=tm, sub=sub),
        grid=(bsz, seq // tm),
        in_specs=[tile, _resident(ln_g), _resident(ln_b)]
                 + [_resident_cols(w_in, off, ATT_GROUP_W) for off in offsets]
                 + [_resident_cols(b_in, off, ATT_GROUP_W) for off in offsets],
        out_specs=[pl.BlockSpec((None, d, tm // d, n), lambda b, i: (b, 0, i, 0)) for _, d in GROUPS]
                  + [tile, tile],
        out_shape=[jax.ShapeDtypeStruct((bsz, d, seq // d, n), jnp.bfloat16) for _, d in GROUPS]
                  + [jax.ShapeDtypeStruct((bsz, seq, dm), jnp.float32),
                     jax.ShapeDtypeStruct((bsz, seq, dm), jnp.bfloat16)],
        scratch_shapes=[pltpu.VMEM((tm // sub, len(REGROUPED), sub, dm), jnp.bfloat16),
                        pltpu.VMEM((tm // sub, dm // LANES, sub, LANES), jnp.float32)],
        compiler_params=pltpu.CompilerParams(
            dimension_semantics=("parallel", "parallel"),
            vmem_limit_bytes=VMEM_LIMIT),
        name="proj_qkv",
    )(x, ln_g, ln_b, *([w_in] * len(offsets)), *([b_in] * len(offsets)))


def _attn_kernel(q_ref, kp_ref, kc_ref, kn_ref, vp_ref, vc_ref, vn_ref, hb_ref, *refs,
                 dil, slopes, sub_len, tq, nres, n_side):
    w_side, b_side = refs[:n_side], refs[n_side:2 * n_side]
    o_ref, st_ref, side_ref, bias_ref = refs[2 * n_side:]
    side_w = w_side[0].shape[1]
    i = pl.program_id(2)
    nq = tq // QBLK

    def window(prev_ref, cur_ref, next_ref, rr, j, lanes):
        lo, hi = j * QBLK - RADIUS, (j + 1) * QBLK + RADIUS
        parts = []
        if lo < 0:
            parts.append(prev_ref[rr, :, lanes])
        parts.append(cur_ref[rr, max(lo, 0):min(hi, tq), lanes])
        if hi > tq:
            parts.append(next_ref[rr, :, lanes])
        return parts[0] if len(parts) == 1 else jnp.concatenate(parts, axis=0)

    @pl.when((pl.program_id(0) == 0) & (pl.program_id(1) == 0) & (i == 0))
    def _():
        row = lax.broadcasted_iota(jnp.int32, (QBLK, KBLK), 0)
        col = lax.broadcasted_iota(jnp.int32, (QBLK, KBLK), 1)
        rel = jnp.abs(col - RADIUS - row)
        band = rel <= RADIUS
        dist = (rel * dil).astype(jnp.float32)
        for var, ok in enumerate((band, band & (col >= RADIUS), band & (col < KBLK - RADIUS))):
            for h in range(HEADS_PER_GROUP):
                bias_ref[var, h] = jnp.where(ok, -slopes[h] * dist, -jnp.inf)

    head_of_lane = lax.broadcasted_iota(jnp.int32, (1, DOT_W), 1) // HEAD_DIM
    f32 = jnp.float32
    last_blk = sub_len // QBLK - 1

    n_parts = HEADS_PER_GROUP // HEADS_PER_DOT
    blocks = [(rr, j) for rr in range(nres) for j in range(nq)]
    scores = {}
    for rr, j in blocks:
        rows = slice(j * QBLK, (j + 1) * QBLK)
        for part in range(n_parts):
            lanes = slice(part * DOT_W, (part + 1) * DOT_W)
            qb = q_ref[rr, rows, lanes]
            kb = window(kp_ref, kc_ref, kn_ref, rr, j, lanes)
            q_heads = jnp.concatenate(
                [jnp.where(head_of_lane == hh, qb, jnp.zeros_like(qb)) for hh in range(HEADS_PER_DOT)],
                axis=0)
            scores[rr, j, part] = lax.dot_general(q_heads, kb, (((1,), (1,)), ((), ())),
                                                  preferred_element_type=f32)
    for k in range(n_side * side_w // MXU_TILE):
        blk_w, lo = divmod(k * MXU_TILE, side_w)
        cols = slice(lo, lo + MXU_TILE)
        acc = (jnp.dot(hb_ref[...], w_side[blk_w][:, cols], preferred_element_type=f32)
               + b_side[blk_w][:, cols])
        side_ref[:, k * MXU_TILE:(k + 1) * MXU_TILE] = acc.astype(side_ref.dtype)
    for rr, j in blocks:
        rows = slice(j * QBLK, (j + 1) * QBLK)
        blk = i * nq + j
        var = jnp.where(blk == 0, 1, jnp.where(blk == last_blk, 2, 0))
        for part in range(n_parts):
            lanes = slice(part * DOT_W, (part + 1) * DOT_W)
            vb = window(vp_ref, vc_ref, vn_ref, rr, j, lanes)
            s_all = scores[rr, j, part]
            p_heads, inv_den, lse = [], [], []
            for hh in range(HEADS_PER_DOT):
                s = s_all[hh * QBLK:(hh + 1) * QBLK, :] + bias_ref[var, part * HEADS_PER_DOT + hh]
                m = jnp.max(s, -1, keepdims=True)
                p = jnp.exp(s - m)
                den = jnp.sum(p, -1, keepdims=True)
                p_heads.append(p.astype(jnp.bfloat16))
                inv_den.append(1.0 / den)
                lse.append(m + jnp.log(den))
            o = jnp.zeros((QBLK, DOT_W), f32)
            st = jnp.zeros((QBLK, DOT_W), f32)
            for hh in range(HEADS_PER_DOT):
                sel = head_of_lane == hh
                o_hh = jnp.dot(p_heads[hh], vb, preferred_element_type=f32)
                o = jnp.where(sel, o_hh * inv_den[hh], o)
                st = jnp.where(sel, lse[hh], st)
            o_ref[rr, rows, lanes] = o.astype(o_ref.dtype)
            st_ref[rr, rows, lanes] = st


def _attention(qkv, hb, w_in, b_in, side_offsets, *, dil, slopes, tq, nres, name):
    bsz, _, sub, _ = qkv.shape
    seq, dm = hb.shape[1:]
    w = ATT_GROUP_W
    nq = tq // RADIUS
    nblk = sub // RADIUS
    n_side = len(side_offsets)
    steps_i = sub // tq
    rows_nat = nres * tq
    assert (dil // nres) * steps_i * rows_nat == seq
    assert nres * (tq // QBLK) * MXU_TILE >= n_side * SIDE_W
    cur = lambda c: pl.BlockSpec((None, nres, tq, w), lambda b, r, i: (b, r, i, c))
    prev = lambda c: pl.BlockSpec((None, nres, RADIUS, w),
                                  lambda b, r, i: (b, r, jnp.maximum(i * nq - 1, 0), c))
    nxt = lambda c: pl.BlockSpec((None, nres, RADIUS, w),
                                 lambda b, r, i: (b, r, jnp.minimum((i + 1) * nq, nblk - 1), c))
    out_spec = pl.BlockSpec((None, nres, tq, w), lambda b, r, i: (b, r, i, 0))
    nat = lambda width: pl.BlockSpec((None, rows_nat, width), lambda b, r, i: (b, r * steps_i + i, 0))
    side_cols = lambda a, off: pl.BlockSpec((a.shape[0], SIDE_W), lambda b, r, i: (0, off // SIDE_W),
                                            pipeline_mode=pl.Buffered(1))
    assert all(off % SIDE_W == 0 for off in side_offsets)
    return pl.pallas_call(
        functools.partial(_attn_kernel, dil=dil, slopes=tuple(float(s) for s in slopes),
                          sub_len=sub, tq=tq, nres=nres, n_side=n_side),
        grid=(bsz, dil // nres, steps_i),
        in_specs=[cur(0), prev(1), cur(1), nxt(1), prev(2), cur(2), nxt(2), nat(dm)]
                 + [side_cols(w_in, off) for off in side_offsets]
                 + [side_cols(b_in, off) for off in side_offsets],
        out_specs=[out_spec, out_spec, nat(n_side * SIDE_W)],
        out_shape=[jax.ShapeDtypeStruct((bsz, dil, sub, w), jnp.bfloat16),
                   jax.ShapeDtypeStruct((bsz, dil, sub, w), jnp.float32),
                   jax.ShapeDtypeStruct((bsz, seq, n_side * SIDE_W), jnp.bfloat16)],
        scratch_shapes=[pltpu.VMEM((3, HEADS_PER_GROUP, QBLK, KBLK), jnp.float32)],
        compiler_params=pltpu.CompilerParams(
            dimension_semantics=("arbitrary", "arbitrary", "arbitrary"),
            vmem_limit_bytes=VMEM_LIMIT),
        name=name,
    )(qkv, qkv, qkv, qkv, qkv, qkv, qkv, hb, *([w_in] * n_side), *([b_in] * n_side))


def _dwconv3_rows(u, prev_row, next_row, w_ref):
    n = u.shape[0]
    row = lax.broadcasted_iota(jnp.int32, u.shape, 0)
    u_dn = jnp.where(row == 0, prev_row, pltpu.roll(u, 1, axis=0))
    u_up = jnp.where(row == n - 1, next_row, pltpu.roll(u, n - 1, axis=0))
    return u_dn * w_ref[0:1, :] + u * w_ref[1:2, :] + u_up * w_ref[2:3, :]


def _to_natural_order(src_ref, slab_ref, tmp_ref):
    dil, n, w = src_ref.shape
    n_slabs = w // LANES
    if dil <= 4:
        for r in range(dil):
            blk = src_ref[r].astype(jnp.float32)
            for c in range(n_slabs):
                slab_ref[c, pl.ds(r, n, stride=dil), :] = blk[:, c * LANES:(c + 1) * LANES]
    else:
        inner = 4
        outer = dil // inner
        assert outer <= 4
        plane = n * outer
        for r in range(dil):
            a, b = divmod(r, inner)
            blk = src_ref[r].astype(jnp.float32)
            for c in range(n_slabs):
                tmp_ref[c, pl.ds(b * plane + a, n, stride=outer), :] = blk[:, c * LANES:(c + 1) * LANES]
        for b in range(inner):
            for c in range(n_slabs):
                slab_ref[c, pl.ds(b, plane, stride=inner), :] = tmp_ref[c, b * plane:(b + 1) * plane, :]
    return jnp.concatenate([slab_ref[c] for c in range(n_slabs)], axis=-1)


def _mix_kernel(h_ref, gb_ref, gc_ref, hn_ref, ga_ref, gbb_ref, gcp_ref, gcn_ref, hnp_ref, hnn_ref,
                o0_ref, o1_ref, o2_ref, s0_ref, s1_ref, s2_ref,
                cw_ref, wa_ref, wb_ref, wo_ref, bo_ref, g1_ref, b1_ref, out_ref,
                slab_o1, slab_s1, slab_o2, slab_s2, tmp_o, tmp_s, *, tm):
    i = pl.program_id(1)
    last = pl.num_programs(1) - 1
    f32, bf16 = jnp.float32, jnp.bfloat16
    dot = functools.partial(jnp.dot, preferred_element_type=f32)

    u = gc_ref[...].astype(f32) * hn_ref[...].astype(f32)
    u_prev = (gcp_ref[...].astype(f32) * hnp_ref[...].astype(f32))[BF16_ROWS - 1:BF16_ROWS, :]
    u_next = (gcn_ref[...].astype(f32) * hnn_ref[...].astype(f32))[0:1, :]
    u_prev = jnp.where(i > 0, u_prev, 0.0)
    u_next = jnp.where(i < last, u_next, 0.0)
    conv = _dwconv3_rows(u, u_prev, u_next, cw_ref)
    y_a = dot((gb_ref[...].astype(f32) * conv).astype(bf16), wa_ref[...])

    o0, s0 = o0_ref[0].astype(f32), s0_ref[0]
    o1, s1 = _to_natural_order(o1_ref, slab_o1, tmp_o), _to_natural_order(s1_ref, slab_s1, tmp_s)
    o2, s2 = _to_natural_order(o2_ref, slab_o2, tmp_o), _to_natural_order(s2_ref, slab_s2, tmp_s)
    m = jnp.maximum(jnp.maximum(s0, s1), s2)
    e0, e1, e2 = jnp.exp(s0 - m), jnp.exp(s1 - m), jnp.exp(s2 - m)
    comb = (e0 * o0 + e1 * o1 + e2 * o2) / (e0 + e1 + e2)
    y_b = dot(comb.astype(bf16), wb_ref[...])

    merged = (jax.nn.sigmoid(ga_ref[...].astype(f32)) * y_a
              + jax.nn.sigmoid(gbb_ref[...].astype(f32)) * y_b).astype(bf16)
    blocks = [slice(s * (tm // 2), (s + 1) * (tm // 2)) for s in range(2)]
    mix = [dot(merged[rows, :], wo_ref[...]) for rows in blocks]
    for rows, part in zip(blocks, mix):
        out_ref[rows, :] = _layer_norm(ALPHA * h_ref[rows, :] + part + bo_ref[...],
                                       g1_ref[...], b1_ref[...])


def _mix(h, sides, outs, stats, weights, *, tm):
    bsz, seq, dm = h.shape
    nhalo = tm // BF16_ROWS
    nrow_blk = seq // BF16_ROWS
    tile = lambda c: pl.BlockSpec((None, tm, dm), lambda b, i: (b, i, c))
    before = lambda c: pl.BlockSpec((None, BF16_ROWS, dm),
                                    lambda b, i: (b, jnp.maximum(i * nhalo - 1, 0), c))
    after = lambda c: pl.BlockSpec((None, BF16_ROWS, dm),
                                   lambda b, i: (b, jnp.minimum((i + 1) * nhalo, nrow_blk - 1), c))
    att = lambda d: pl.BlockSpec((None, d, tm // d, ATT_GROUP_W), lambda b, i: (b, 0, i, 0))
    att_specs = [att(d) for _, d in GROUPS]
    bc, ha, gb2 = sides
    slab = pltpu.VMEM((ATT_GROUP_W // LANES, tm, LANES), jnp.float32)
    return pl.pallas_call(
        functools.partial(_mix_kernel, tm=tm),
        grid=(bsz, seq // tm),
        in_specs=[tile(0), tile(0), tile(1), tile(0), tile(1), tile(0),
                  before(1), after(1), before(0), after(0)]
                 + att_specs + att_specs + [_resident(a) for a in weights],
        out_specs=tile(0),
        out_shape=jax.ShapeDtypeStruct((bsz, seq, dm), jnp.float32),
        scratch_shapes=[slab] * 6,
        compiler_params=pltpu.CompilerParams(
            dimension_semantics=("parallel", "parallel"),
            vmem_limit_bytes=VMEM_LIMIT),
        name="mix",
    )(h, bc, bc, ha, ha, gb2, bc, bc, ha, ha, *outs, *stats, *weights)


def _ffn_kernel(h_ref, hp_ref, hn_ref, wu_ref, bu_ref, cw_ref, cb_ref, wd_ref, bd_ref,
                g_ref, b_ref, out_ref, f_ref, *, tm, sub, chunks):
    i = pl.program_id(1)
    last = pl.num_programs(1) - 1
    f32 = jnp.float32
    h = h_ref[...]
    lhs_all = jnp.concatenate([h, hp_ref[...], hn_ref[...]], axis=0).astype(jnp.bfloat16)
    lhs = lhs_all[0:tm, :]
    lo = 0
    for ck in chunks:
        cols = slice(lo, lo + ck)
        gcols = slice(D_FF + lo, D_FF + lo + ck)
        a_all = jnp.dot(lhs_all, wu_ref[:, cols], preferred_element_type=f32) + bu_ref[:, cols]
        gate = jnp.dot(lhs, wu_ref[:, gcols], preferred_element_type=f32) + bu_ref[:, gcols]
        a = a_all[0:tm, :]
        a_prev = jnp.where(i > 0, a_all[tm + F32_ROWS - 1:tm + F32_ROWS, :], 0.0)
        a_next = jnp.where(i < last, a_all[tm + F32_ROWS:tm + F32_ROWS + 1, :], 0.0)
        pre = _dwconv3_rows(a, a_prev, a_next, cw_ref.at[:, cols]) + cb_ref[:, cols]
        f = 0.5 * pre * (1.0 + lax.erf(pre * (2.0 ** -0.5))) * gate
        f_ref[:, cols] = f.astype(jnp.bfloat16)
        lo += ck
    blocks = [slice(s * sub, (s + 1) * sub) for s in range(tm // sub)]
    ffn = [jnp.dot(f_ref[rows, :], wd_ref[...], preferred_element_type=f32) for rows in blocks]
    for rows, part in zip(blocks, ffn):
        out_ref[rows, :] = _layer_norm(ALPHA * h[rows, :] + part + bd_ref[...], g_ref[...], b_ref[...])


def _ffn(h1, wu, bu, cw, cb, wd, bd, g, b, *, tm, sub, chunks):
    bsz, seq, dm = h1.shape
    assert sum(chunks) == D_FF
    nhalo = tm // F32_ROWS
    nrow_blk = seq // F32_ROWS
    weights = (wu, bu, cw, cb, wd, bd, g, b)
    return pl.pallas_call(
        functools.partial(_ffn_kernel, tm=tm, sub=sub, chunks=chunks),
        grid=(bsz, seq // tm),
        in_specs=[
            pl.BlockSpec((None, tm, dm), lambda bb, i: (bb, i, 0)),
            pl.BlockSpec((None, F32_ROWS, dm), lambda bb, i: (bb, jnp.maximum(i * nhalo - 1, 0), 0)),
            pl.BlockSpec((None, F32_ROWS, dm),
                         lambda bb, i: (bb, jnp.minimum((i + 1) * nhalo, nrow_blk - 1), 0)),
        ] + [_resident(a) for a in weights],
        out_specs=pl.BlockSpec((None, tm, dm), lambda bb, i: (bb, i, 0)),
        out_shape=jax.ShapeDtypeStruct((bsz, seq, dm), jnp.float32),
        scratch_shapes=[pltpu.VMEM((tm, D_FF), jnp.bfloat16)],
        compiler_params=pltpu.CompilerParams(
            dimension_semantics=("parallel", "parallel"),
            vmem_limit_bytes=VMEM_LIMIT),
        name="ffn",
    )(h1, h1, h1, *weights)


def kernel(x, ln0_g, ln0_b, w_in, b_in, conv_w, w_a, w_b, w_o, b_o, ln1_g, ln1_b,
           w_up, b_up, ffn_conv_w, ffn_conv_b, w_down, b_down, ln2_g, ln2_b):
    bf16 = jnp.bfloat16
    row = lambda v: v.reshape(1, -1)
    slopes = _alibi_slopes()
    g0, b0 = row(ln0_g), row(ln0_b)
    bsz, seq, _ = x.shape
    l = 0
    wi, bi = w_in[l].astype(bf16), row(b_in[l])
    *qkvs, h, hb = _qkv_proj(x, g0, b0, wi, bi, tm=1024, sub=256)
    side_cols = ((OFF_B, OFF_C), (OFF_H, OFF_GA), (OFF_GB,))
    outs, stats, sides = [], [], []
    for g, (_, dil) in enumerate(GROUPS):
        tq = min(ATTN_ROWS_PER_STEP, seq // dil)
        offsets = [off + k * SIDE_W for off in side_cols[g] for k in range(D_MODEL // SIDE_W)]
        o, st, side = _attention(qkvs[g], hb, wi, bi, offsets, dil=dil, slopes=slopes[g], tq=tq,
                                 nres=ATTN_ROWS_PER_STEP // tq, name=f"attn{g}")
        outs.append(o)
        stats.append(st)
        sides.append(side)
    mix_weights = (conv_w[l], w_a[l].astype(bf16), w_b[l].astype(bf16), w_o[l].astype(bf16),
                   row(b_o[l]), row(ln1_g[l]), row(ln1_b[l]))
    h1 = _mix(h, sides, outs, stats, mix_weights, tm=512)
    return _ffn(h1, w_up[l].astype(bf16), row(b_up[l]), ffn_conv_w[l], row(ffn_conv_b[l]),
                w_down[l].astype(bf16), row(b_down[l]), row(ln2_g[l]), row(ln2_b[l]),
                tm=1024, sub=512, chunks=_chunks(D_FF, 2 * MXU_TILE))
```

```python
import functools

import numpy as np
import jax
import jax.numpy as jnp
from jax import lax
from jax.experimental import pallas as pl
from jax.experimental.pallas import tpu as pltpu

D_MODEL = 1024
D_CONV = D_MODEL
HEAD_DIM = 64
HEADS_PER_GROUP = 8
GROUPS = ((128, 1), (512, 4), (2048, 16))
N_GROUPS = len(GROUPS)
REGROUPED = tuple(g for g, (_, d) in enumerate(GROUPS) if d > 1)
N_ATT_HEADS = N_GROUPS * HEADS_PER_GROUP
ATT_GROUP_W = HEADS_PER_GROUP * HEAD_DIM
ATT_QKV_W = N_GROUPS * ATT_GROUP_W
D_FF = 2816
LN_EPS = 1e-5
DEPTH = 1
assert DEPTH == 1
ALPHA = (2.0 * DEPTH) ** 0.25

OFF_B = 0
OFF_C = OFF_B + D_CONV
OFF_H = OFF_C + D_CONV
OFF_Q = OFF_H + D_CONV
OFF_K = OFF_Q + ATT_QKV_W
OFF_V = OFF_K + ATT_QKV_W
OFF_GA = OFF_V + ATT_QKV_W
OFF_GB = OFF_GA + D_MODEL

RADIUS = 64
QBLK = 2 * RADIUS
KBLK = QBLK + 2 * RADIUS
LANES = 128
BF16_ROWS = 16
F32_ROWS = 8
HEADS_PER_DOT = 4
DOT_W = HEADS_PER_DOT * HEAD_DIM
MXU_TILE = 256
SIDE_W = 512
ATTN_ROWS_PER_STEP = 1024
VMEM_LIMIT = 60000 * 1024

assert all(w // (2 * d) == RADIUS for w, d in GROUPS)
assert D_FF % MXU_TILE == 0


def _chunks(total, width):
    full = [width] * (total // width)
    rem = total - sum(full)
    return tuple(full + ([rem] if rem else []))


def _layer_norm(xf, g, b):
    mu = jnp.mean(xf, -1, keepdims=True)
    xc = xf - mu
    var = jnp.mean(xc * xc, -1, keepdims=True)
    return xc * lax.rsqrt(var + LN_EPS) * g + b


def _alibi_slopes():
    n = N_ATT_HEADS
    return np.exp2(-8.0 * np.arange(1, n + 1, dtype=np.float64) / n).reshape(N_GROUPS, HEADS_PER_GROUP)


def _resident(a):
    return pl.BlockSpec(a.shape, lambda b, i: (0,) * a.ndim, pipeline_mode=pl.Buffered(1))


def _resident_cols(a, offset, width):
    assert offset % width == 0
    return pl.BlockSpec((a.shape[0], width), lambda b, i: (0, offset // width),
                        pipeline_mode=pl.Buffered(1))


def _qkv_proj_kernel(x_ref, g_ref, b_ref, *refs, tm, sub):
    n_w = 3 * N_GROUPS
    w_refs, bias_refs = refs[:n_w], refs[n_w:2 * n_w]
    o_refs = refs[2 * n_w:2 * n_w + N_GROUPS]
    h_ref, hb_out_ref, hb_ref, slab_ref = refs[2 * n_w + N_GROUPS:]
    n_slabs = D_MODEL // LANES
    for s in range(tm // sub):
        rows = slice(s * sub, (s + 1) * sub)
        h = _layer_norm(x_ref[rows, :], g_ref[...], b_ref[...])
        hb = h.astype(jnp.bfloat16)
        h_ref[rows, :] = h
        hb_out_ref[rows, :] = hb
        for c in range(n_slabs):
            slab_ref[s, c] = h[:, c * LANES:(c + 1) * LANES]
        for g, (_, dil) in enumerate(GROUPS):
            n = sub // dil
            if dil == 1:
                lhs = hb
            else:
                slot = REGROUPED.index(g)
                for r in range(dil):
                    for c in range(n_slabs):
                        hb_ref[s, slot, r * n:(r + 1) * n, c * LANES:(c + 1) * LANES] = (
                            slab_ref[s, c, pl.ds(r, n, stride=dil), :].astype(jnp.bfloat16))
                lhs = hb_ref[s, slot]
            for t in range(3):
                acc = (jnp.dot(lhs, w_refs[3 * g + t][...], preferred_element_type=jnp.float32)
                       + bias_refs[3 * g + t][...])
                if t == 0:
                    acc = acc * HEAD_DIM ** -0.5
                cols = slice(t * ATT_GROUP_W, (t + 1) * ATT_GROUP_W)
                for r in range(dil):
                    o_refs[g][r, s * n:(s + 1) * n, cols] = (
                        acc[r * n:(r + 1) * n, :].astype(o_refs[g].dtype))


def _qkv_proj(x, ln_g, ln_b, w_in, b_in, *, tm, sub):
    bsz, seq, dm = x.shape
    n = 3 * ATT_GROUP_W
    offsets = [off + g * ATT_GROUP_W for g in range(N_GROUPS) for off in (OFF_Q, OFF_K, OFF_V)]
    tile = pl.BlockSpec((None, tm, dm), lambda b, i: (b, i, 0))
    return pl.pallas_call(
        functools.partial(_qkv_proj_kernel, tm=tm, sub=sub),
        grid=(bsz, seq // tm),
        in_specs=[tile, _resident(ln_g), _resident(ln_b)]
                 + [_resident_cols(w_in, off, ATT_GROUP_W) for off in offsets]
                 + [_resident_cols(b_in, off, ATT_GROUP_W) for off in offsets],
        out_specs=[pl.BlockSpec((None, d, tm // d, n), lambda b, i: (b, 0, i, 0)) for _, d in GROUPS]
                  + [tile, tile],
        out_shape=[jax.ShapeDtypeStruct((bsz, d, seq // d, n), jnp.bfloat16) for _, d in GROUPS]
                  + [jax.ShapeDtypeStruct((bsz, seq, dm), jnp.float32),
                     jax.ShapeDtypeStruct((bsz, seq, dm), jnp.bfloat16)],
        scratch_shapes=[pltpu.VMEM((tm // sub, len(REGROUPED), sub, dm), jnp.bfloat16),
                        pltpu.VMEM((tm // sub, dm // LANES, sub, LANES), jnp.float32)],
        compiler_params=pltpu.CompilerParams(
            dimension_semantics=("parallel", "parallel"),
            vmem_limit_bytes=VMEM_LIMIT),
        name="proj_qkv",
    )(x, ln_g, ln_b, *([w_in] * len(offsets)), *([b_in] * len(offsets)))


def _attn_kernel(q_ref, kp_ref, kc_ref, kn_ref, vp_ref, vc_ref, vn_ref, hb_ref, *refs,
                 dil, slopes, sub_len, tq, nres, n_side, n_cast):
    w_side, b_side = refs[:n_side], refs[n_side:2 * n_side]
    cast_in = refs[2 * n_side:2 * n_side + n_cast]
    o_ref, st_ref, side_ref = refs[2 * n_side + n_cast:2 * n_side + n_cast + 3]
    cast_out = refs[2 * n_side + n_cast + 3:2 * n_side + 2 * n_cast + 3]
    bias_ref = refs[-1]
    for src, dst in zip(cast_in, cast_out):
        dst[...] = src[...].astype(dst.dtype)
    side_w = w_side[0].shape[1]
    i = pl.program_id(2)
    nq = tq // QBLK

    def window(prev_ref, cur_ref, next_ref, rr, j, lanes):
        lo, hi = j * QBLK - RADIUS, (j + 1) * QBLK + RADIUS
        parts = []
        if lo < 0:
            parts.append(prev_ref[rr, :, lanes])
        parts.append(cur_ref[rr, max(lo, 0):min(hi, tq), lanes])
        if hi > tq:
            parts.append(next_ref[rr, :, lanes])
        return parts[0] if len(parts) == 1 else jnp.concatenate(parts, axis=0)

    @pl.when((pl.program_id(0) == 0) & (pl.program_id(1) == 0) & (i == 0))
    def _():
        row = lax.broadcasted_iota(jnp.int32, (QBLK, KBLK), 0)
        col = lax.broadcasted_iota(jnp.int32, (QBLK, KBLK), 1)
        rel = jnp.abs(col - RADIUS - row)
        band = rel <= RADIUS
        dist = (rel * dil).astype(jnp.float32)
        for var, ok in enumerate((band, band & (col >= RADIUS), band & (col < KBLK - RADIUS))):
            for h in range(HEADS_PER_GROUP):
                bias_ref[var, h] = jnp.where(ok, -slopes[h] * dist, -jnp.inf)

    head_of_lane = lax.broadcasted_iota(jnp.int32, (1, DOT_W), 1) // HEAD_DIM
    f32 = jnp.float32
    last_blk = sub_len // QBLK - 1

    n_parts = HEADS_PER_GROUP // HEADS_PER_DOT
    blocks = [(rr, j) for rr in range(nres) for j in range(nq)]
    scores = {}
    for rr, j in blocks:
        rows = slice(j * QBLK, (j + 1) * QBLK)
        for part in range(n_parts):
            lanes = slice(part * DOT_W, (part + 1) * DOT_W)
            qb = q_ref[rr, rows, lanes]
            kb = window(kp_ref, kc_ref, kn_ref, rr, j, lanes)
            q_heads = jnp.concatenate(
                [jnp.where(head_of_lane == hh, qb, jnp.zeros_like(qb)) for hh in range(HEADS_PER_DOT)],
                axis=0)
            scores[rr, j, part] = lax.dot_general(q_heads, kb, (((1,), (1,)), ((), ())),
                                                  preferred_element_type=f32)
    for k in range(n_side * side_w // MXU_TILE):
        blk_w, lo = divmod(k * MXU_TILE, side_w)
        cols = slice(lo, lo + MXU_TILE)
        acc = (jnp.dot(hb_ref[...], w_side[blk_w][:, cols], preferred_element_type=f32)
               + b_side[blk_w][:, cols])
        side_ref[:, k * MXU_TILE:(k + 1) * MXU_TILE] = acc.astype(side_ref.dtype)
    for rr, j in blocks:
        rows = slice(j * QBLK, (j + 1) * QBLK)
        blk = i * nq + j
        var = jnp.where(blk == 0, 1, jnp.where(blk == last_blk, 2, 0))
        for part in range(n_parts):
            lanes = slice(part * DOT_W, (part + 1) * DOT_W)
            vb = window(vp_ref, vc_ref, vn_ref, rr, j, lanes)
            s_all = scores[rr, j, part]
            p_heads, inv_den, lse = [], [], []
            for hh in range(HEADS_PER_DOT):
                s = s_all[hh * QBLK:(hh + 1) * QBLK, :] + bias_ref[var, part * HEADS_PER_DOT + hh]
                m = jnp.max(s, -1, keepdims=True)
                p = jnp.exp(s - m)
                den = jnp.sum(p, -1, keepdims=True)
                p_heads.append(p.astype(jnp.bfloat16))
                inv_den.append(1.0 / den)
                lse.append(m + jnp.log(den))
            o = jnp.zeros((QBLK, DOT_W), f32)
            st = jnp.zeros((QBLK, DOT_W), f32)
            for hh in range(HEADS_PER_DOT):
                sel = head_of_lane == hh
                o_hh = jnp.dot(p_heads[hh], vb, preferred_element_type=f32)
                o = jnp.where(sel, o_hh * inv_den[hh], o)
                st = jnp.where(sel, lse[hh], st)
            o_ref[rr, rows, lanes] = o.astype(o_ref.dtype)
            st_ref[rr, rows, lanes] = st


def _attention(qkv, hb, w_in, b_in, side_offsets, to_cast, *, dil, slopes, tq, nres, name):
    bsz, _, sub, _ = qkv.shape
    seq, dm = hb.shape[1:]
    w = ATT_GROUP_W
    nq = tq // RADIUS
    nblk = sub // RADIUS
    n_side = len(side_offsets)
    steps_i = sub // tq
    steps_r = dil // nres
    n_steps = bsz * steps_r * steps_i
    rows_nat = nres * tq
    assert steps_r * steps_i * rows_nat == seq
    assert nres * (tq // QBLK) * MXU_TILE >= n_side * SIDE_W
    assert all(a.shape[0] % (n_steps * BF16_ROWS) == 0 for a in to_cast)
    slab = lambda a: pl.BlockSpec((a.shape[0] // n_steps, a.shape[1]),
                                  lambda b, r, i: ((b * steps_r + r) * steps_i + i, 0))
    cur = lambda c: pl.BlockSpec((None, nres, tq, w), lambda b, r, i: (b, r, i, c))
    prev = lambda c: pl.BlockSpec((None, nres, RADIUS, w),
                                  lambda b, r, i: (b, r, jnp.maximum(i * nq - 1, 0), c))
    nxt = lambda c: pl.BlockSpec((None, nres, RADIUS, w),
                                 lambda b, r, i: (b, r, jnp.minimum((i + 1) * nq, nblk - 1), c))
    out_spec = pl.BlockSpec((None, nres, tq, w), lambda b, r, i: (b, r, i, 0))
    nat = lambda width: pl.BlockSpec((None, rows_nat, width), lambda b, r, i: (b, r * steps_i + i, 0))
    side_cols = lambda a, off: pl.BlockSpec((a.shape[0], SIDE_W), lambda b, r, i: (0, off // SIDE_W),
                                            pipeline_mode=pl.Buffered(1))
    assert all(off % SIDE_W == 0 for off in side_offsets)
    return pl.pallas_call(
        functools.partial(_attn_kernel, dil=dil, slopes=tuple(float(s) for s in slopes),
                          sub_len=sub, tq=tq, nres=nres, n_side=n_side, n_cast=len(to_cast)),
        grid=(bsz, steps_r, steps_i),
        in_specs=[cur(0), prev(1), cur(1), nxt(1), prev(2), cur(2), nxt(2), nat(dm)]
                 + [side_cols(w_in, off) for off in side_offsets]
                 + [side_cols(b_in, off) for off in side_offsets]
                 + [slab(a) for a in to_cast],
        out_specs=[out_spec, out_spec, nat(n_side * SIDE_W)] + [slab(a) for a in to_cast],
        out_shape=[jax.ShapeDtypeStruct((bsz, dil, sub, w), jnp.bfloat16),
                   jax.ShapeDtypeStruct((bsz, dil, sub, w), jnp.float32),
                   jax.ShapeDtypeStruct((bsz, seq, n_side * SIDE_W), jnp.bfloat16)]
                  + [jax.ShapeDtypeStruct(a.shape, jnp.bfloat16) for a in to_cast],
        scratch_shapes=[pltpu.VMEM((3, HEADS_PER_GROUP, QBLK, KBLK), jnp.float32)],
        compiler_params=pltpu.CompilerParams(
            dimension_semantics=("arbitrary", "arbitrary", "arbitrary"),
            vmem_limit_bytes=VMEM_LIMIT),
        name=name,
    )(qkv, qkv, qkv, qkv, qkv, qkv, qkv, hb, *([w_in] * n_side), *([b_in] * n_side), *to_cast)


def _dwconv3_rows(u, prev_row, next_row, w_ref):
    n = u.shape[0]
    row = lax.broadcasted_iota(jnp.int32, u.shape, 0)
    u_dn = jnp.where(row == 0, prev_row, pltpu.roll(u, 1, axis=0))
    u_up = jnp.where(row == n - 1, next_row, pltpu.roll(u, n - 1, axis=0))
    return u_dn * w_ref[0:1, :] + u * w_ref[1:2, :] + u_up * w_ref[2:3, :]


def _to_natural_order(src_ref, slab_ref, tmp_ref):
    dil, n, w = src_ref.shape
    n_slabs = w // LANES
    if dil <= 4:
        for r in range(dil):
            blk = src_ref[r].astype(jnp.float32)
            for c in range(n_slabs):
                slab_ref[c, pl.ds(r, n, stride=dil), :] = blk[:, c * LANES:(c + 1) * LANES]
    else:
        inner = 4
        outer = dil // inner
        assert outer <= 4
        plane = n * outer
        for r in range(dil):
            a, b = divmod(r, inner)
            blk = src_ref[r].astype(jnp.float32)
            for c in range(n_slabs):
                tmp_ref[c, pl.ds(b * plane + a, n, stride=outer), :] = blk[:, c * LANES:(c + 1) * LANES]
        for b in range(inner):
            for c in range(n_slabs):
                slab_ref[c, pl.ds(b, plane, stride=inner), :] = tmp_ref[c, b * plane:(b + 1) * plane, :]
    return jnp.concatenate([slab_ref[c] for c in range(n_slabs)], axis=-1)


def _mix_kernel(h_ref, gb_ref, gc_ref, hn_ref, ga_ref, gbb_ref, gcp_ref, gcn_ref, hnp_ref, hnn_ref,
                o0_ref, o1_ref, o2_ref, s0_ref, s1_ref, s2_ref,
                cw_ref, wa_ref, wb_ref, wo_ref, bo_ref, g1_ref, b1_ref, out_ref,
                slab_o1, slab_s1, slab_o2, slab_s2, tmp_o, tmp_s, *, tm):
    i = pl.program_id(1)
    last = pl.num_programs(1) - 1
    f32, bf16 = jnp.float32, jnp.bfloat16
    dot = functools.partial(jnp.dot, preferred_element_type=f32)

    u = gc_ref[...].astype(f32) * hn_ref[...].astype(f32)
    u_prev = (gcp_ref[...].astype(f32) * hnp_ref[...].astype(f32))[BF16_ROWS - 1:BF16_ROWS, :]
    u_next = (gcn_ref[...].astype(f32) * hnn_ref[...].astype(f32))[0:1, :]
    u_prev = jnp.where(i > 0, u_prev, 0.0)
    u_next = jnp.where(i < last, u_next, 0.0)
    conv = _dwconv3_rows(u, u_prev, u_next, cw_ref)
    y_a = dot((gb_ref[...].astype(f32) * conv).astype(bf16), wa_ref[...])

    o0, s0 = o0_ref[0].astype(f32), s0_ref[0]
    o1, s1 = _to_natural_order(o1_ref, slab_o1, tmp_o), _to_natural_order(s1_ref, slab_s1, tmp_s)
    o2, s2 = _to_natural_order(o2_ref, slab_o2, tmp_o), _to_natural_order(s2_ref, slab_s2, tmp_s)
    m = jnp.maximum(jnp.maximum(s0, s1), s2)
    e0, e1, e2 = jnp.exp(s0 - m), jnp.exp(s1 - m), jnp.exp(s2 - m)
    comb = (e0 * o0 + e1 * o1 + e2 * o2) / (e0 + e1 + e2)
    y_b = dot(comb.astype(bf16), wb_ref[...])

    merged = (jax.nn.sigmoid(ga_ref[...].astype(f32)) * y_a
              + jax.nn.sigmoid(gbb_ref[...].astype(f32)) * y_b).astype(bf16)
    blocks = [slice(s * (tm // 2), (s + 1) * (tm // 2)) for s in range(2)]
    mix = [dot(merged[rows, :], wo_ref[...]) for rows in blocks]
    for rows, part in zip(blocks, mix):
        out_ref[rows, :] = _layer_norm(ALPHA * h_ref[rows, :] + part + bo_ref[...],
                                       g1_ref[...], b1_ref[...])


def _mix(h, sides, outs, stats, weights, *, tm):
    bsz, seq, dm = h.shape
    nhalo = tm // BF16_ROWS
    nrow_blk = seq // BF16_ROWS
    tile = lambda c: pl.BlockSpec((None, tm, dm), lambda b, i: (b, i, c))
    before = lambda c: pl.BlockSpec((None, BF16_ROWS, dm),
                                    lambda b, i: (b, jnp.maximum(i * nhalo - 1, 0), c))
    after = lambda c: pl.BlockSpec((None, BF16_ROWS, dm),
                                   lambda b, i: (b, jnp.minimum((i + 1) * nhalo, nrow_blk - 1), c))
    att = lambda d: pl.BlockSpec((None, d, tm // d, ATT_GROUP_W), lambda b, i: (b, 0, i, 0))
    att_specs = [att(d) for _, d in GROUPS]
    bc, ha, gb2 = sides
    slab = pltpu.VMEM((ATT_GROUP_W // LANES, tm, LANES), jnp.float32)
    return pl.pallas_call(
        functools.partial(_mix_kernel, tm=tm),
        grid=(bsz, seq // tm),
        in_specs=[tile(0), tile(0), tile(1), tile(0), tile(1), tile(0),
                  before(1), after(1), before(0), after(0)]
                 + att_specs + att_specs + [_resident(a) for a in weights],
        out_specs=tile(0),
        out_shape=jax.ShapeDtypeStruct((bsz, seq, dm), jnp.float32),
        scratch_shapes=[slab] * 6,
        compiler_params=pltpu.CompilerParams(
            dimension_semantics=("parallel", "parallel"),
            vmem_limit_bytes=VMEM_LIMIT),
        name="mix",
    )(h, bc, bc, ha, ha, gb2, bc, bc, ha, ha, *outs, *stats, *weights)


def _ffn_kernel(h_ref, hp_ref, hn_ref, wu_ref, bu_ref, cw_ref, cb_ref, wd_ref, bd_ref,
                g_ref, b_ref, out_ref, f_ref, *, tm, sub, chunks):
    i = pl.program_id(1)
    last = pl.num_programs(1) - 1
    f32 = jnp.float32
    h = h_ref[...]
    lhs_all = jnp.concatenate([h, hp_ref[...], hn_ref[...]], axis=0).astype(jnp.bfloat16)
    lhs = lhs_all[0:tm, :]
    lo = 0
    for ck in chunks:
        cols = slice(lo, lo + ck)
        gcols = slice(D_FF + lo, D_FF + lo + ck)
        a_all = jnp.dot(lhs_all, wu_ref[:, cols], preferred_element_type=f32) + bu_ref[:, cols]
        gate = jnp.dot(lhs, wu_ref[:, gcols], preferred_element_type=f32) + bu_ref[:, gcols]
        a = a_all[0:tm, :]
        a_prev = jnp.where(i > 0, a_all[tm + F32_ROWS - 1:tm + F32_ROWS, :], 0.0)
        a_next = jnp.where(i < last, a_all[tm + F32_ROWS:tm + F32_ROWS + 1, :], 0.0)
        pre = _dwconv3_rows(a, a_prev, a_next, cw_ref.at[:, cols]) + cb_ref[:, cols]
        f = 0.5 * pre * (1.0 + lax.erf(pre * (2.0 ** -0.5))) * gate
        f_ref[:, cols] = f.astype(jnp.bfloat16)
        lo += ck
    blocks = [slice(s * sub, (s + 1) * sub) for s in range(tm // sub)]
    ffn = [jnp.dot(f_ref[rows, :], wd_ref[...], preferred_element_type=f32) for rows in blocks]
    for rows, part in zip(blocks, ffn):
        out_ref[rows, :] = _layer_norm(ALPHA * h[rows, :] + part + bd_ref[...], g_ref[...], b_ref[...])


def _ffn(h1, wu, bu, cw, cb, wd, bd, g, b, *, tm, sub, chunks):
    bsz, seq, dm = h1.shape
    assert sum(chunks) == D_FF
    nhalo = tm // F32_ROWS
    nrow_blk = seq // F32_ROWS
    weights = (wu, bu, cw, cb, wd, bd, g, b)
    return pl.pallas_call(
        functools.partial(_ffn_kernel, tm=tm, sub=sub, chunks=chunks),
        grid=(bsz, seq // tm),
        in_specs=[
            pl.BlockSpec((None, tm, dm), lambda bb, i: (bb, i, 0)),
            pl.BlockSpec((None, F32_ROWS, dm), lambda bb, i: (bb, jnp.maximum(i * nhalo - 1, 0), 0)),
            pl.BlockSpec((None, F32_ROWS, dm),
                         lambda bb, i: (bb, jnp.minimum((i + 1) * nhalo, nrow_blk - 1), 0)),
        ] + [_resident(a) for a in weights],
        out_specs=pl.BlockSpec((None, tm, dm), lambda bb, i: (bb, i, 0)),
        out_shape=jax.ShapeDtypeStruct((bsz, seq, dm), jnp.float32),
        scratch_shapes=[pltpu.VMEM((tm, D_FF), jnp.bfloat16)],
        compiler_params=pltpu.CompilerParams(
            dimension_semantics=("parallel", "parallel"),
            vmem_limit_bytes=VMEM_LIMIT),
        name="ffn",
    )(h1, h1, h1, *weights)


def kernel(x, ln0_g, ln0_b, w_in, b_in, conv_w, w_a, w_b, w_o, b_o, ln1_g, ln1_b,
           w_up, b_up, ffn_conv_w, ffn_conv_b, w_down, b_down, ln2_g, ln2_b):
    bf16 = jnp.bfloat16
    row = lambda v: v.reshape(1, -1)
    slopes = _alibi_slopes()
    g0, b0 = row(ln0_g), row(ln0_b)
    bsz, seq, _ = x.shape
    l = 0
    wi, bi = w_in[l].astype(bf16), row(b_in[l])
    *qkvs, h, hb = _qkv_proj(x, g0, b0, wi, bi, tm=1024, sub=256)
    side_cols = ((OFF_B, OFF_C), (OFF_H, OFF_GA), (OFF_GB,))
    to_cast = ((w_up[l],), (w_down[l], w_a[l]), (w_b[l], w_o[l]))
    outs, stats, sides, cast = [], [], [], []
    for g, (_, dil) in enumerate(GROUPS):
        tq = min(ATTN_ROWS_PER_STEP, seq // dil)
        offsets = [off + k * SIDE_W for off in side_cols[g] for k in range(D_MODEL // SIDE_W)]
        o, st, side, *casted = _attention(qkvs[g], hb, wi, bi, offsets, to_cast[g], dil=dil,
                                          slopes=slopes[g], tq=tq, nres=ATTN_ROWS_PER_STEP // tq,
                                          name=f"attn{g}")
        outs.append(o)
        stats.append(st)
        sides.append(side)
        cast.extend(casted)
    w_up_b, w_down_b, w_a_b, w_b_b, w_o_b = cast
    mix_weights = (conv_w[l], w_a_b, w_b_b, w_o_b, row(b_o[l]), row(ln1_g[l]), row(ln1_b[l]))
    h1 = _mix(h, sides, outs, stats, mix_weights, tm=512)
    return _ffn(h1, w_up_b, row(b_up[l]), ffn_conv_w[l], row(ffn_conv_b[l]),
                w_down_b, row(b_down[l]), row(ln2_g[l]), row(ln2_b[l]),
                tm=1024, sub=512, chunks=_chunks(D_FF, 2 * MXU_TILE))
```

```python
import functools

import numpy as np
import jax
import jax.numpy as jnp
from jax import lax
from jax.experimental import pallas as pl
from jax.experimental.pallas import tpu as pltpu

D_MODEL = 1024
D_CONV = D_MODEL
HEAD_DIM = 64
HEADS_PER_GROUP = 8
GROUPS = ((128, 1), (512, 4), (2048, 16))
N_GROUPS = len(GROUPS)
REGROUPED = tuple(g for g, (_, d) in enumerate(GROUPS) if d > 1)
N_ATT_HEADS = N_GROUPS * HEADS_PER_GROUP
ATT_GROUP_W = HEADS_PER_GROUP * HEAD_DIM
ATT_QKV_W = N_GROUPS * ATT_GROUP_W
D_FF = 2816
LN_EPS = 1e-5
DEPTH = 1
assert DEPTH == 1
ALPHA = (2.0 * DEPTH) ** 0.25

OFF_B = 0
OFF_C = OFF_B + D_CONV
OFF_H = OFF_C + D_CONV
OFF_Q = OFF_H + D_CONV
OFF_K = OFF_Q + ATT_QKV_W
OFF_V = OFF_K + ATT_QKV_W
OFF_GA = OFF_V + ATT_QKV_W
OFF_GB = OFF_GA + D_MODEL

RADIUS = 64
QBLK = 2 * RADIUS
KBLK = QBLK + 2 * RADIUS
LANES = 128
BF16_ROWS = 16
F32_ROWS = 8
HEADS_PER_DOT = 4
DOT_W = HEADS_PER_DOT * HEAD_DIM
MXU_TILE = 256
SIDE_W = 512
ATTN_ROWS_PER_STEP = 1024
VMEM_LIMIT = 60000 * 1024

assert all(w // (2 * d) == RADIUS for w, d in GROUPS)
assert D_FF % MXU_TILE == 0


def _chunks(total, width):
    full = [width] * (total // width)
    rem = total - sum(full)
    return tuple(full + ([rem] if rem else []))


def _layer_norm(xf, g, b):
    mu = jnp.mean(xf, -1, keepdims=True)
    xc = xf - mu
    var = jnp.mean(xc * xc, -1, keepdims=True)
    return xc * lax.rsqrt(var + LN_EPS) * g + b


def _alibi_slopes():
    n = N_ATT_HEADS
    return np.exp2(-8.0 * np.arange(1, n + 1, dtype=np.float64) / n).reshape(N_GROUPS, HEADS_PER_GROUP)


def _resident(a):
    return pl.BlockSpec(a.shape, lambda b, i: (0,) * a.ndim, pipeline_mode=pl.Buffered(1))


def _resident_cols(a, offset, width):
    assert offset % width == 0
    return pl.BlockSpec((a.shape[0], width), lambda b, i: (0, offset // width),
                        pipeline_mode=pl.Buffered(1))


def _qkv_proj_kernel(x_ref, g_ref, b_ref, *refs, tm, sub):
    n_w = 3 * N_GROUPS
    w_refs, bias_refs = refs[:n_w], refs[n_w:2 * n_w]
    o_refs = refs[2 * n_w:2 * n_w + N_GROUPS]
    h_ref, hb_out_ref, hb_ref, slab_ref = refs[2 * n_w + N_GROUPS:]
    n_slabs = D_MODEL // LANES
    for s in range(tm // sub):
        rows = slice(s * sub, (s + 1) * sub)
        h = _layer_norm(x_ref[rows, :], g_ref[...], b_ref[...])
        hb = h.astype(jnp.bfloat16)
        h_ref[rows, :] = h
        hb_out_ref[rows, :] = hb
        for c in range(n_slabs):
            slab_ref[s, c] = h[:, c * LANES:(c + 1) * LANES]
        for g, (_, dil) in enumerate(GROUPS):
            n = sub // dil
            if dil == 1:
                lhs = hb
            else:
                slot = REGROUPED.index(g)
                for r in range(dil):
                    for c in range(n_slabs):
                        hb_ref[s, slot, r * n:(r + 1) * n, c * LANES:(c + 1) * LANES] = (
                            slab_ref[s, c, pl.ds(r, n, stride=dil), :].astype(jnp.bfloat16))
                lhs = hb_ref[s, slot]
            for t in range(3):
                acc = (jnp.dot(lhs, w_refs[3 * g + t][...], preferred_element_type=jnp.float32)
                       + bias_refs[3 * g + t][...])
                if t == 0:
                    acc = acc * HEAD_DIM ** -0.5
                cols = slice(t * ATT_GROUP_W, (t + 1) * ATT_GROUP_W)
                for r in range(dil):
                    o_refs[g][r, s * n:(s + 1) * n, cols] = (
                        acc[r * n:(r + 1) * n, :].astype(o_refs[g].dtype))


def _qkv_proj(x, ln_g, ln_b, w_in, b_in, *, tm, sub):
    bsz, seq, dm = x.shape
    n = 3 * ATT_GROUP_W
    offsets = [off + g * ATT_GROUP_W for g in range(N_GROUPS) for off in (OFF_Q, OFF_K, OFF_V)]
    tile = pl.BlockSpec((None, tm, dm), lambda b, i: (b, i, 0))
    return pl.pallas_call(
        functools.partial(_qkv_proj_kernel, tm=tm, sub=sub),
        grid=(bsz, seq // tm),
        in_specs=[tile, _resident(ln_g), _resident(ln_b)]
                 + [_resident_cols(w_in, off, ATT_GROUP_W) for off in offsets]
                 + [_resident_cols(b_in, off, ATT_GROUP_W) for off in offsets],
        out_specs=[pl.BlockSpec((None, d, tm // d, n), lambda b, i: (b, 0, i, 0)) for _, d in GROUPS]
                  + [tile, tile],
        out_shape=[jax.ShapeDtypeStruct((bsz, d, seq // d, n), jnp.bfloat16) for _, d in GROUPS]
                  + [jax.ShapeDtypeStruct((bsz, seq, dm), jnp.float32),
                     jax.ShapeDtypeStruct((bsz, seq, dm), jnp.bfloat16)],
        scratch_shapes=[pltpu.VMEM((tm // sub, len(REGROUPED), sub, dm), jnp.bfloat16),
                        pltpu.VMEM((tm // sub, dm // LANES, sub, LANES), jnp.float32)],
        compiler_params=pltpu.CompilerParams(
            dimension_semantics=("parallel", "parallel"),
            vmem_limit_bytes=VMEM_LIMIT),
        name="proj_qkv",
    )(x, ln_g, ln_b, *([w_in] * len(offsets)), *([b_in] * len(offsets)))


def _attn_kernel(q_ref, kp_ref, kc_ref, kn_ref, vp_ref, vc_ref, vn_ref, hb_ref, *refs,
                 dil, slopes, sub_len, tq, nres, n_side, n_cast):
    w_side, b_side = refs[:n_side], refs[n_side:2 * n_side]
    cast_in = refs[2 * n_side:2 * n_side + n_cast]
    o_ref, st_ref, side_ref = refs[2 * n_side + n_cast:2 * n_side + n_cast + 3]
    cast_out = refs[2 * n_side + n_cast + 3:2 * n_side + 2 * n_cast + 3]
    bias_ref = refs[-1]
    for src, dst in zip(cast_in, cast_out):
        dst[...] = src[...].astype(dst.dtype)
    side_w = w_side[0].shape[1]
    i = pl.program_id(2)
    nq = tq // QBLK

    def window(prev_ref, cur_ref, next_ref, rr, j, lanes):
        lo, hi = j * QBLK - RADIUS, (j + 1) * QBLK + RADIUS
        parts = []
        if lo < 0:
            parts.append(prev_ref[rr, :, lanes])
        parts.append(cur_ref[rr, max(lo, 0):min(hi, tq), lanes])
        if hi > tq:
            parts.append(next_ref[rr, :, lanes])
        return parts[0] if len(parts) == 1 else jnp.concatenate(parts, axis=0)

    @pl.when((pl.program_id(0) == 0) & (pl.program_id(1) == 0) & (i == 0))
    def _():
        row = lax.broadcasted_iota(jnp.int32, (QBLK, KBLK), 0)
        col = lax.broadcasted_iota(jnp.int32, (QBLK, KBLK), 1)
        rel = jnp.abs(col - RADIUS - row)
        band = rel <= RADIUS
        dist = (rel * dil).astype(jnp.float32)
        for var, ok in enumerate((band, band & (col >= RADIUS), band & (col < KBLK - RADIUS))):
            for h in range(HEADS_PER_GROUP):
                bias_ref[var, h] = jnp.where(ok, -slopes[h] * dist, -jnp.inf)

    head_of_lane = lax.broadcasted_iota(jnp.int32, (1, DOT_W), 1) // HEAD_DIM
    f32 = jnp.float32
    last_blk = sub_len // QBLK - 1

    n_parts = HEADS_PER_GROUP // HEADS_PER_DOT
    blocks = [(rr, j) for rr in range(nres) for j in range(nq)]
    scores = {}
    for rr, j in blocks:
        rows = slice(j * QBLK, (j + 1) * QBLK)
        for part in range(n_parts):
            lanes = slice(part * DOT_W, (part + 1) * DOT_W)
            qb = q_ref[rr, rows, lanes]
            kb = window(kp_ref, kc_ref, kn_ref, rr, j, lanes)
            q_heads = jnp.concatenate(
                [jnp.where(head_of_lane == hh, qb, jnp.zeros_like(qb)) for hh in range(HEADS_PER_DOT)],
                axis=0)
            scores[rr, j, part] = lax.dot_general(q_heads, kb, (((1,), (1,)), ((), ())),
                                                  preferred_element_type=f32)
    for k in range(n_side * side_w // MXU_TILE):
        blk_w, lo = divmod(k * MXU_TILE, side_w)
        cols = slice(lo, lo + MXU_TILE)
        acc = (jnp.dot(hb_ref[...], w_side[blk_w][:, cols], preferred_element_type=f32)
               + b_side[blk_w][:, cols])
        side_ref[:, k * MXU_TILE:(k + 1) * MXU_TILE] = acc.astype(side_ref.dtype)
    for rr, j in blocks:
        rows = slice(j * QBLK, (j + 1) * QBLK)
        blk = i * nq + j
        var = jnp.where(blk == 0, 1, jnp.where(blk == last_blk, 2, 0))
        for part in range(n_parts):
            lanes = slice(part * DOT_W, (part + 1) * DOT_W)
            vb = window(vp_ref, vc_ref, vn_ref, rr, j, lanes)
            s_all = scores[rr, j, part]
            p_heads, inv_den, lse = [], [], []
            for hh in range(HEADS_PER_DOT):
                s = s_all[hh * QBLK:(hh + 1) * QBLK, :] + bias_ref[var, part * HEADS_PER_DOT + hh]
                m = jnp.max(s, -1, keepdims=True)
                p = jnp.exp(s - m)
                den = jnp.sum(p, -1, keepdims=True)
                p_heads.append(p.astype(jnp.bfloat16))
                inv_den.append(1.0 / den)
                lse.append(m + jnp.log(den))
            o = jnp.zeros((QBLK, DOT_W), f32)
            st = jnp.zeros((QBLK, DOT_W), f32)
            for hh in range(HEADS_PER_DOT):
                sel = head_of_lane == hh
                o_hh = jnp.dot(p_heads[hh], vb, preferred_element_type=f32)
                o = jnp.where(sel, o_hh * inv_den[hh], o)
                st = jnp.where(sel, lse[hh], st)
            o_ref[rr, rows, lanes] = o.astype(o_ref.dtype)
            st_ref[rr, rows, lanes] = st


def _attention(qkv, hb, w_in, b_in, side_offsets, to_cast, *, dil, slopes, tq, nres, name):
    bsz, _, sub, _ = qkv.shape
    seq, dm = hb.shape[1:]
    w = ATT_GROUP_W
    nq = tq // RADIUS
    nblk = sub // RADIUS
    n_side = len(side_offsets)
    steps_i = sub // tq
    steps_r = dil // nres
    n_steps = bsz * steps_r * steps_i
    rows_nat = nres * tq
    assert steps_r * steps_i * rows_nat == seq
    assert nres * (tq // QBLK) * MXU_TILE >= n_side * SIDE_W
    assert all(a.shape[0] % (n_steps * BF16_ROWS) == 0 for a in to_cast)
    slab = lambda a: pl.BlockSpec((a.shape[0] // n_steps, a.shape[1]),
                                  lambda b, r, i: ((b * steps_r + r) * steps_i + i, 0))
    cur = lambda c: pl.BlockSpec((None, nres, tq, w), lambda b, r, i: (b, r, i, c))
    prev = lambda c: pl.BlockSpec((None, nres, RADIUS, w),
                                  lambda b, r, i: (b, r, jnp.maximum(i * nq - 1, 0), c))
    nxt = lambda c: pl.BlockSpec((None, nres, RADIUS, w),
                                 lambda b, r, i: (b, r, jnp.minimum((i + 1) * nq, nblk - 1), c))
    out_spec = pl.BlockSpec((None, nres, tq, w), lambda b, r, i: (b, r, i, 0))
    nat = lambda width: pl.BlockSpec((None, rows_nat, width), lambda b, r, i: (b, r * steps_i + i, 0))
    side_cols = lambda a, off: pl.BlockSpec((a.shape[0], SIDE_W), lambda b, r, i: (0, off // SIDE_W),
                                            pipeline_mode=pl.Buffered(1))
    assert all(off % SIDE_W == 0 for off in side_offsets)
    return pl.pallas_call(
        functools.partial(_attn_kernel, dil=dil, slopes=tuple(float(s) for s in slopes),
                          sub_len=sub, tq=tq, nres=nres, n_side=n_side, n_cast=len(to_cast)),
        grid=(bsz, steps_r, steps_i),
        in_specs=[cur(0), prev(1), cur(1), nxt(1), prev(2), cur(2), nxt(2), nat(dm)]
                 + [side_cols(w_in, off) for off in side_offsets]
                 + [side_cols(b_in, off) for off in side_offsets]
                 + [slab(a) for a in to_cast],
        out_specs=[out_spec, out_spec, nat(n_side * SIDE_W)] + [slab(a) for a in to_cast],
        out_shape=[jax.ShapeDtypeStruct((bsz, dil, sub, w), jnp.bfloat16),
                   jax.ShapeDtypeStruct((bsz, dil, sub, w), jnp.float32),
                   jax.ShapeDtypeStruct((bsz, seq, n_side * SIDE_W), jnp.bfloat16)]
                  + [jax.ShapeDtypeStruct(a.shape, jnp.bfloat16) for a in to_cast],
        scratch_shapes=[pltpu.VMEM((3, HEADS_PER_GROUP, QBLK, KBLK), jnp.float32)],
        compiler_params=pltpu.CompilerParams(
            dimension_semantics=("arbitrary", "arbitrary", "arbitrary"),
            vmem_limit_bytes=VMEM_LIMIT),
        name=name,
    )(qkv, qkv, qkv, qkv, qkv, qkv, qkv, hb, *([w_in] * n_side), *([b_in] * n_side), *to_cast)


def _dwconv3_rows(u, prev_row, next_row, w_ref):
    n = u.shape[0]
    row = lax.broadcasted_iota(jnp.int32, u.shape, 0)
    u_dn = jnp.where(row == 0, prev_row, pltpu.roll(u, 1, axis=0))
    u_up = jnp.where(row == n - 1, next_row, pltpu.roll(u, n - 1, axis=0))
    return u_dn * w_ref[0:1, :] + u * w_ref[1:2, :] + u_up * w_ref[2:3, :]


def _to_natural_order(src_ref, slab_ref, tmp_ref):
    dil, n, w = src_ref.shape
    n_slabs = w // LANES
    if dil <= 4:
        for r in range(dil):
            blk = src_ref[r].astype(jnp.float32)
            for c in range(n_slabs):
                slab_ref[c, pl.ds(r, n, stride=dil), :] = blk[:, c * LANES:(c + 1) * LANES]
    else:
        inner = 4
        outer = dil // inner
        assert outer <= 4
        plane = n * outer
        for r in range(dil):
            a, b = divmod(r, inner)
            blk = src_ref[r].astype(jnp.float32)
            for c in range(n_slabs):
                tmp_ref[c, pl.ds(b * plane + a, n, stride=outer), :] = blk[:, c * LANES:(c + 1) * LANES]
        for b in range(inner):
            for c in range(n_slabs):
                slab_ref[c, pl.ds(b, plane, stride=inner), :] = tmp_ref[c, b * plane:(b + 1) * plane, :]
    return jnp.concatenate([slab_ref[c] for c in range(n_slabs)], axis=-1)


def _mix_kernel(h_ref, gb_ref, gc_ref, hn_ref, ga_ref, gbb_ref, gcp_ref, gcn_ref, hnp_ref, hnn_ref,
                o0_ref, o1_ref, o2_ref, s0_ref, s1_ref, s2_ref,
                cw_ref, wa_ref, wb_ref, wo_ref, bo_ref, g1_ref, b1_ref, out_ref,
                slab_o1, slab_s1, slab_o2, slab_s2, tmp_o, tmp_s, *, tm):
    i = pl.program_id(1)
    last = pl.num_programs(1) - 1
    f32, bf16 = jnp.float32, jnp.bfloat16
    dot = functools.partial(jnp.dot, preferred_element_type=f32)

    u = gc_ref[...].astype(f32) * hn_ref[...].astype(f32)
    u_prev = (gcp_ref[...].astype(f32) * hnp_ref[...].astype(f32))[BF16_ROWS - 1:BF16_ROWS, :]
    u_next = (gcn_ref[...].astype(f32) * hnn_ref[...].astype(f32))[0:1, :]
    u_prev = jnp.where(i > 0, u_prev, 0.0)
    u_next = jnp.where(i < last, u_next, 0.0)
    conv = _dwconv3_rows(u, u_prev, u_next, cw_ref)
    y_a = dot((gb_ref[...].astype(f32) * conv).astype(bf16), wa_ref[...])

    o0, s0 = o0_ref[0].astype(f32), s0_ref[0]
    o1, s1 = _to_natural_order(o1_ref, slab_o1, tmp_o), _to_natural_order(s1_ref, slab_s1, tmp_s)
    o2, s2 = _to_natural_order(o2_ref, slab_o2, tmp_o), _to_natural_order(s2_ref, slab_s2, tmp_s)
    m = jnp.maximum(jnp.maximum(s0, s1), s2)
    e0, e1, e2 = jnp.exp(s0 - m), jnp.exp(s1 - m), jnp.exp(s2 - m)
    comb = (e0 * o0 + e1 * o1 + e2 * o2) / (e0 + e1 + e2)
    y_b = dot(comb.astype(bf16), wb_ref[...])

    merged = (jax.nn.sigmoid(ga_ref[...].astype(f32)) * y_a
              + jax.nn.sigmoid(gbb_ref[...].astype(f32)) * y_b).astype(bf16)
    blocks = [slice(s * (tm // 2), (s + 1) * (tm // 2)) for s in range(2)]
    mix = [dot(merged[rows, :], wo_ref[...]) for rows in blocks]
    for rows, part in zip(blocks, mix):
        out_ref[rows, :] = _layer_norm(ALPHA * h_ref[rows, :] + part + bo_ref[...],
                                       g1_ref[...], b1_ref[...])


def _mix(h, sides, outs, stats, weights, *, tm):
    bsz, seq, dm = h.shape
    nhalo = tm // BF16_ROWS
    nrow_blk = seq // BF16_ROWS
    tile = lambda c: pl.BlockSpec((None, tm, dm), lambda b, i: (b, i, c))
    before = lambda c: pl.BlockSpec((None, BF16_ROWS, dm),
                                    lambda b, i: (b, jnp.maximum(i * nhalo - 1, 0), c))
    after = lambda c: pl.BlockSpec((None, BF16_ROWS, dm),
                                   lambda b, i: (b, jnp.minimum((i + 1) * nhalo, nrow_blk - 1), c))
    att = lambda d: pl.BlockSpec((None, d, tm // d, ATT_GROUP_W), lambda b, i: (b, 0, i, 0))
    att_specs = [att(d) for _, d in GROUPS]
    bc, ha, gb2 = sides
    slab = pltpu.VMEM((ATT_GROUP_W // LANES, tm, LANES), jnp.float32)
    return pl.pallas_call(
        functools.partial(_mix_kernel, tm=tm),
        grid=(bsz, seq // tm),
        in_specs=[tile(0), tile(0), tile(1), tile(0), tile(1), tile(0),
                  before(1), after(1), before(0), after(0)]
                 + att_specs + att_specs + [_resident(a) for a in weights],
        out_specs=tile(0),
        out_shape=jax.ShapeDtypeStruct((bsz, seq, dm), jnp.float32),
        scratch_shapes=[slab] * 6,
        compiler_params=pltpu.CompilerParams(
            dimension_semantics=("parallel", "parallel"),
            vmem_limit_bytes=VMEM_LIMIT),
        name="mix",
    )(h, bc, bc, ha, ha, gb2, bc, bc, ha, ha, *outs, *stats, *weights)


def _ffn_kernel(h_ref, hp_ref, hn_ref, wu_ref, bu_ref, cw_ref, cb_ref, wd_ref, bd_ref,
                g_ref, b_ref, out_ref, f_ref, *, tm, sub, chunks):
    i = pl.program_id(1)
    last = pl.num_programs(1) - 1
    f32 = jnp.float32
    h = h_ref[...]
    lhs_all = jnp.concatenate([h, hp_ref[...], hn_ref[...]], axis=0).astype(jnp.bfloat16)
    lhs = lhs_all[0:tm, :]
    lo = 0
    for ck in chunks:
        cols = slice(lo, lo + ck)
        gcols = slice(D_FF + lo, D_FF + lo + ck)
        a_all = jnp.dot(lhs_all, wu_ref[:, cols], preferred_element_type=f32) + bu_ref[:, cols]
        gate = jnp.dot(lhs, wu_ref[:, gcols], preferred_element_type=f32) + bu_ref[:, gcols]
        a = a_all[0:tm, :]
        a_prev = jnp.where(i > 0, a_all[tm + F32_ROWS - 1:tm + F32_ROWS, :], 0.0)
        a_next = jnp.where(i < last, a_all[tm + F32_ROWS:tm + F32_ROWS + 1, :], 0.0)
        pre = _dwconv3_rows(a, a_prev, a_next, cw_ref.at[:, cols]) + cb_ref[:, cols]
        f = 0.5 * pre * (1.0 + lax.erf(pre * (2.0 ** -0.5))) * gate
        f_ref[:, cols] = f.astype(jnp.bfloat16)
        lo += ck
    blocks = [slice(0, tm - sub), slice(tm - sub, tm)]
    ffn = [jnp.dot(f_ref[rows, :], wd_ref[...], preferred_element_type=f32) for rows in blocks]
    for rows, part in zip(blocks, ffn):
        out_ref[rows, :] = _layer_norm(ALPHA * h[rows, :] + part + bd_ref[...], g_ref[...], b_ref[...])


def _ffn(h1, wu, bu, cw, cb, wd, bd, g, b, *, tm, sub, chunks):
    bsz, seq, dm = h1.shape
    assert sum(chunks) == D_FF
    nhalo = tm // F32_ROWS
    nrow_blk = seq // F32_ROWS
    weights = (wu, bu, cw, cb, wd, bd, g, b)
    return pl.pallas_call(
        functools.partial(_ffn_kernel, tm=tm, sub=sub, chunks=chunks),
        grid=(bsz, seq // tm),
        in_specs=[
            pl.BlockSpec((None, tm, dm), lambda bb, i: (bb, i, 0)),
            pl.BlockSpec((None, F32_ROWS, dm), lambda bb, i: (bb, jnp.maximum(i * nhalo - 1, 0), 0)),
            pl.BlockSpec((None, F32_ROWS, dm),
                         lambda bb, i: (bb, jnp.minimum((i + 1) * nhalo, nrow_blk - 1), 0)),
        ] + [_resident(a) for a in weights],
        out_specs=pl.BlockSpec((None, tm, dm), lambda bb, i: (bb, i, 0)),
        out_shape=jax.ShapeDtypeStruct((bsz, seq, dm), jnp.float32),
        scratch_shapes=[pltpu.VMEM((tm, D_FF), jnp.bfloat16)],
        compiler_params=pltpu.CompilerParams(
            dimension_semantics=("parallel", "parallel"),
            vmem_limit_bytes=VMEM_LIMIT),
        name="ffn",
    )(h1, h1, h1, *weights)


def kernel(x, ln0_g, ln0_b, w_in, b_in, conv_w, w_a, w_b, w_o, b_o, ln1_g, ln1_b,
           w_up, b_up, ffn_conv_w, ffn_conv_b, w_down, b_down, ln2_g, ln2_b):
    bf16 = jnp.bfloat16
    row = lambda v: v.reshape(1, -1)
    slopes = _alibi_slopes()
    g0, b0 = row(ln0_g), row(ln0_b)
    bsz, seq, _ = x.shape
    l = 0
    wi, bi = w_in[l].astype(bf16), row(b_in[l])
    *qkvs, h, hb = _qkv_proj(x, g0, b0, wi, bi, tm=1024, sub=256)
    side_cols = ((OFF_B, OFF_C), (OFF_H, OFF_GA), (OFF_GB,))
    to_cast = ((w_up[l],), (w_down[l], w_a[l]), (w_b[l], w_o[l]))
    outs, stats, sides, cast = [], [], [], []
    for g, (_, dil) in enumerate(GROUPS):
        tq = min(ATTN_ROWS_PER_STEP, seq // dil)
        offsets = [off + k * SIDE_W for off in side_cols[g] for k in range(D_MODEL // SIDE_W)]
        o, st, side, *casted = _attention(qkvs[g], hb, wi, bi, offsets, to_cast[g], dil=dil,
                                          slopes=slopes[g], tq=tq, nres=ATTN_ROWS_PER_STEP // tq,
                                          name=f"attn{g}")
        outs.append(o)
        stats.append(st)
        sides.append(side)
        cast.extend(casted)
    w_up_b, w_down_b, w_a_b, w_b_b, w_o_b = cast
    mix_weights = (conv_w[l], w_a_b, w_b_b, w_o_b, row(b_o[l]), row(ln1_g[l]), row(ln1_b[l]))
    h1 = _mix(h, sides, outs, stats, mix_weights, tm=512)
    return _ffn(h1, w_up_b, row(b_up[l]), ffn_conv_w[l], row(ffn_conv_b[l]),
                w_down_b, row(b_down[l]), row(ln2_g[l]), row(ln2_b[l]),
                tm=1024, sub=256, chunks=_chunks(D_FF, 2 * MXU_TILE))
```

```python
import functools

import numpy as np
import jax
import jax.numpy as jnp
from jax import lax
from jax.experimental import pallas as pl
from jax.experimental.pallas import tpu as pltpu

D_MODEL = 1024
D_CONV = D_MODEL
HEAD_DIM = 64
HEADS_PER_GROUP = 8
GROUPS = ((128, 1), (512, 4), (2048, 16))
N_GROUPS = len(GROUPS)
REGROUPED = tuple(g for g, (_, d) in enumerate(GROUPS) if d > 1)
N_ATT_HEADS = N_GROUPS * HEADS_PER_GROUP
ATT_GROUP_W = HEADS_PER_GROUP * HEAD_DIM
ATT_QKV_W = N_GROUPS * ATT_GROUP_W
D_FF = 2816
LN_EPS = 1e-5
DEPTH = 1
assert DEPTH == 1
ALPHA = (2.0 * DEPTH) ** 0.25

OFF_B = 0
OFF_C = OFF_B + D_CONV
OFF_H = OFF_C + D_CONV
OFF_Q = OFF_H + D_CONV
OFF_K = OFF_Q + ATT_QKV_W
OFF_V = OFF_K + ATT_QKV_W
OFF_GA = OFF_V + ATT_QKV_W
OFF_GB = OFF_GA + D_MODEL

RADIUS = 64
QBLK = 2 * RADIUS
KBLK = QBLK + 2 * RADIUS
LANES = 128
BF16_ROWS = 16
F32_ROWS = 8
HEADS_PER_DOT = 4
DOT_W = HEADS_PER_DOT * HEAD_DIM
MXU_TILE = 256
SIDE_W = 512
ATTN_ROWS_PER_STEP = 1024
VMEM_LIMIT = 60000 * 1024

assert all(w // (2 * d) == RADIUS for w, d in GROUPS)
assert D_FF % MXU_TILE == 0


def _chunks(total, width):
    full = [width] * (total // width)
    rem = total - sum(full)
    return tuple(full + ([rem] if rem else []))


def _layer_norm(xf, g, b):
    mu = jnp.mean(xf, -1, keepdims=True)
    xc = xf - mu
    var = jnp.mean(xc * xc, -1, keepdims=True)
    return xc * lax.rsqrt(var + LN_EPS) * g + b


def _alibi_slopes():
    n = N_ATT_HEADS
    return np.exp2(-8.0 * np.arange(1, n + 1, dtype=np.float64) / n).reshape(N_GROUPS, HEADS_PER_GROUP)


def _resident(a):
    return pl.BlockSpec(a.shape, lambda b, i: (0,) * a.ndim, pipeline_mode=pl.Buffered(1))


def _resident_cols(a, offset, width):
    assert offset % width == 0
    return pl.BlockSpec((a.shape[0], width), lambda b, i: (0, offset // width),
                        pipeline_mode=pl.Buffered(1))


def _qkv_proj_kernel(x_ref, g_ref, b_ref, *refs, tm, sub, n_cast):
    n_w = 3 * N_GROUPS
    w_refs, bias_refs = refs[:n_w], refs[n_w:2 * n_w]
    cast_in = refs[2 * n_w:2 * n_w + n_cast]
    o_refs = refs[2 * n_w + n_cast:2 * n_w + n_cast + N_GROUPS]
    h_ref, hb_out_ref = refs[2 * n_w + n_cast + N_GROUPS:2 * n_w + n_cast + N_GROUPS + 2]
    cast_out = refs[2 * n_w + n_cast + N_GROUPS + 2:2 * n_w + 2 * n_cast + N_GROUPS + 2]
    hb_ref, slab_ref = refs[-2:]
    for src, dst in zip(cast_in, cast_out):
        dst[...] = src[...].astype(dst.dtype)
    n_slabs = D_MODEL // LANES
    for s in range(tm // sub):
        rows = slice(s * sub, (s + 1) * sub)
        buf = s % hb_ref.shape[0]
        h = _layer_norm(x_ref[rows, :], g_ref[...], b_ref[...])
        hb = h.astype(jnp.bfloat16)
        h_ref[rows, :] = h
        hb_out_ref[rows, :] = hb
        for c in range(n_slabs):
            slab_ref[buf, c] = h[:, c * LANES:(c + 1) * LANES]
        for g, (_, dil) in enumerate(GROUPS):
            n = sub // dil
            if dil == 1:
                lhs = hb
            else:
                slot = REGROUPED.index(g)
                for r in range(dil):
                    for c in range(n_slabs):
                        hb_ref[buf, slot, r * n:(r + 1) * n, c * LANES:(c + 1) * LANES] = (
                            slab_ref[buf, c, pl.ds(r, n, stride=dil), :].astype(jnp.bfloat16))
                lhs = hb_ref[buf, slot]
            for t in range(3):
                acc = (jnp.dot(lhs, w_refs[3 * g + t][...], preferred_element_type=jnp.float32)
                       + bias_refs[3 * g + t][...])
                if t == 0:
                    acc = acc * HEAD_DIM ** -0.5
                cols = slice(t * ATT_GROUP_W, (t + 1) * ATT_GROUP_W)
                for r in range(dil):
                    o_refs[g][r, s * n:(s + 1) * n, cols] = (
                        acc[r * n:(r + 1) * n, :].astype(o_refs[g].dtype))


def _qkv_proj(x, ln_g, ln_b, w_qkv, b_in, w_in_f32, cast_offsets, *, tm, sub):
    bsz, seq, dm = x.shape
    n = 3 * ATT_GROUP_W
    offsets = [off + g * ATT_GROUP_W for g in range(N_GROUPS) for off in (OFF_Q, OFF_K, OFF_V)]
    tile = pl.BlockSpec((None, tm, dm), lambda b, i: (b, i, 0))
    steps_i = seq // tm
    n_steps = bsz * steps_i
    slab_rows = dm // n_steps
    assert slab_rows * n_steps == dm and slab_rows % BF16_ROWS == 0
    assert all(off % SIDE_W == 0 for off in cast_offsets)
    slab_in = lambda off: pl.BlockSpec((slab_rows, SIDE_W), lambda b, i: (b * steps_i + i, off // SIDE_W))
    slab_out = pl.BlockSpec((slab_rows, SIDE_W), lambda b, i: (b * steps_i + i, 0))
    return pl.pallas_call(
        functools.partial(_qkv_proj_kernel, tm=tm, sub=sub, n_cast=len(cast_offsets)),
        grid=(bsz, steps_i),
        in_specs=[tile, _resident(ln_g), _resident(ln_b)]
                 + [_resident_cols(w_qkv, off - OFF_Q, ATT_GROUP_W) for off in offsets]
                 + [_resident_cols(b_in, off, ATT_GROUP_W) for off in offsets]
                 + [slab_in(off) for off in cast_offsets],
        out_specs=[pl.BlockSpec((None, d, tm // d, n), lambda b, i: (b, 0, i, 0)) for _, d in GROUPS]
                  + [tile, tile] + [slab_out] * len(cast_offsets),
        out_shape=[jax.ShapeDtypeStruct((bsz, d, seq // d, n), jnp.bfloat16) for _, d in GROUPS]
                  + [jax.ShapeDtypeStruct((bsz, seq, dm), jnp.float32),
                     jax.ShapeDtypeStruct((bsz, seq, dm), jnp.bfloat16)]
                  + [jax.ShapeDtypeStruct((dm, SIDE_W), jnp.bfloat16)] * len(cast_offsets),
        scratch_shapes=[pltpu.VMEM((2, len(REGROUPED), sub, dm), jnp.bfloat16),
                        pltpu.VMEM((2, dm // LANES, sub, LANES), jnp.float32)],
        compiler_params=pltpu.CompilerParams(
            dimension_semantics=("arbitrary", "arbitrary"),
            vmem_limit_bytes=VMEM_LIMIT),
        name="proj_qkv",
    )(x, ln_g, ln_b, *([w_qkv] * len(offsets)), *([b_in] * len(offsets)),
      *([w_in_f32] * len(cast_offsets)))


def _attn_kernel(q_ref, kp_ref, kc_ref, kn_ref, vp_ref, vc_ref, vn_ref, hb_ref, *refs,
                 dil, slopes, sub_len, tq, nres, n_side, n_cast):
    w_side, b_side = refs[:n_side], refs[n_side:2 * n_side]
    cast_in = refs[2 * n_side:2 * n_side + n_cast]
    o_ref, st_ref, side_ref = refs[2 * n_side + n_cast:2 * n_side + n_cast + 3]
    cast_out = refs[2 * n_side + n_cast + 3:2 * n_side + 2 * n_cast + 3]
    bias_ref = refs[-1]
    for src, dst in zip(cast_in, cast_out):
        dst[...] = src[...].astype(dst.dtype)
    side_w = w_side[0].shape[1]
    i = pl.program_id(2)
    nq = tq // QBLK

    def window(prev_ref, cur_ref, next_ref, rr, j, lanes):
        lo, hi = j * QBLK - RADIUS, (j + 1) * QBLK + RADIUS
        parts = []
        if lo < 0:
            parts.append(prev_ref[rr, :, lanes])
        parts.append(cur_ref[rr, max(lo, 0):min(hi, tq), lanes])
        if hi > tq:
            parts.append(next_ref[rr, :, lanes])
        return parts[0] if len(parts) == 1 else jnp.concatenate(parts, axis=0)

    @pl.when((pl.program_id(0) == 0) & (pl.program_id(1) == 0) & (i == 0))
    def _():
        row = lax.broadcasted_iota(jnp.int32, (QBLK, KBLK), 0)
        col = lax.broadcasted_iota(jnp.int32, (QBLK, KBLK), 1)
        rel = jnp.abs(col - RADIUS - row)
        band = rel <= RADIUS
        dist = (rel * dil).astype(jnp.float32)
        for var, ok in enumerate((band, band & (col >= RADIUS), band & (col < KBLK - RADIUS))):
            for h in range(HEADS_PER_GROUP):
                bias_ref[var, h] = jnp.where(ok, -slopes[h] * dist, -jnp.inf)

    head_of_lane = lax.broadcasted_iota(jnp.int32, (1, DOT_W), 1) // HEAD_DIM
    f32 = jnp.float32
    last_blk = sub_len // QBLK - 1

    n_parts = HEADS_PER_GROUP // HEADS_PER_DOT
    blocks = [(rr, j) for rr in range(nres) for j in range(nq)]
    scores = {}
    for rr, j in blocks:
        rows = slice(j * QBLK, (j + 1) * QBLK)
        for part in range(n_parts):
            lanes = slice(part * DOT_W, (part + 1) * DOT_W)
            qb = q_ref[rr, rows, lanes]
            kb = window(kp_ref, kc_ref, kn_ref, rr, j, lanes)
            q_heads = jnp.concatenate(
                [jnp.where(head_of_lane == hh, qb, jnp.zeros_like(qb)) for hh in range(HEADS_PER_DOT)],
                axis=0)
            scores[rr, j, part] = lax.dot_general(q_heads, kb, (((1,), (1,)), ((), ())),
                                                  preferred_element_type=f32)
    for k in range(n_side * side_w // MXU_TILE):
        blk_w, lo = divmod(k * MXU_TILE, side_w)
        cols = slice(lo, lo + MXU_TILE)
        acc = (jnp.dot(hb_ref[...], w_side[blk_w][:, cols], preferred_element_type=f32)
               + b_side[blk_w][:, cols])
        side_ref[:, k * MXU_TILE:(k + 1) * MXU_TILE] = acc.astype(side_ref.dtype)
    for rr, j in blocks:
        rows = slice(j * QBLK, (j + 1) * QBLK)
        blk = i * nq + j
        var = jnp.where(blk == 0, 1, jnp.where(blk == last_blk, 2, 0))
        for part in range(n_parts):
            lanes = slice(part * DOT_W, (part + 1) * DOT_W)
            vb = window(vp_ref, vc_ref, vn_ref, rr, j, lanes)
            s_all = scores[rr, j, part]
            p_heads, inv_den, lse = [], [], []
            for hh in range(HEADS_PER_DOT):
                s = s_all[hh * QBLK:(hh + 1) * QBLK, :] + bias_ref[var, part * HEADS_PER_DOT + hh]
                m = jnp.max(s, -1, keepdims=True)
                p = jnp.exp(s - m)
                den = jnp.sum(p, -1, keepdims=True)
                p_heads.append(p.astype(jnp.bfloat16))
                inv_den.append(1.0 / den)
                lse.append(m + jnp.log(den))
            o = jnp.zeros((QBLK, DOT_W), f32)
            st = jnp.zeros((QBLK, DOT_W), f32)
            for hh in range(HEADS_PER_DOT):
                sel = head_of_lane == hh
                o_hh = jnp.dot(p_heads[hh], vb, preferred_element_type=f32)
                o = jnp.where(sel, o_hh * inv_den[hh], o)
                st = jnp.where(sel, lse[hh], st)
            o_ref[rr, rows, lanes] = o.astype(o_ref.dtype)
            st_ref[rr, rows, lanes] = st


def _attention(qkv, hb, side_w, b_in, side_offsets, to_cast, *, dil, slopes, tq, nres, name):
    bsz, _, sub, _ = qkv.shape
    seq, dm = hb.shape[1:]
    w = ATT_GROUP_W
    nq = tq // RADIUS
    nblk = sub // RADIUS
    n_side = len(side_offsets)
    steps_i = sub // tq
    steps_r = dil // nres
    n_steps = bsz * steps_r * steps_i
    rows_nat = nres * tq
    assert steps_r * steps_i * rows_nat == seq
    assert nres * (tq // QBLK) * MXU_TILE >= n_side * SIDE_W
    assert all(a.shape[0] % (n_steps * BF16_ROWS) == 0 for a in to_cast)
    slab = lambda a: pl.BlockSpec((a.shape[0] // n_steps, a.shape[1]),
                                  lambda b, r, i: ((b * steps_r + r) * steps_i + i, 0))
    cur = lambda c: pl.BlockSpec((None, nres, tq, w), lambda b, r, i: (b, r, i, c))
    prev = lambda c: pl.BlockSpec((None, nres, RADIUS, w),
                                  lambda b, r, i: (b, r, jnp.maximum(i * nq - 1, 0), c))
    nxt = lambda c: pl.BlockSpec((None, nres, RADIUS, w),
                                 lambda b, r, i: (b, r, jnp.minimum((i + 1) * nq, nblk - 1), c))
    out_spec = pl.BlockSpec((None, nres, tq, w), lambda b, r, i: (b, r, i, 0))
    nat = lambda width: pl.BlockSpec((None, rows_nat, width), lambda b, r, i: (b, r * steps_i + i, 0))
    side_cols = lambda a, off: pl.BlockSpec((a.shape[0], SIDE_W), lambda b, r, i: (0, off // SIDE_W),
                                            pipeline_mode=pl.Buffered(1))
    assert all(off % SIDE_W == 0 for off in side_offsets)
    return pl.pallas_call(
        functools.partial(_attn_kernel, dil=dil, slopes=tuple(float(s) for s in slopes),
                          sub_len=sub, tq=tq, nres=nres, n_side=n_side, n_cast=len(to_cast)),
        grid=(bsz, steps_r, steps_i),
        in_specs=[cur(0), prev(1), cur(1), nxt(1), prev(2), cur(2), nxt(2), nat(dm)]
                 + [pl.BlockSpec(a.shape, lambda b, r, i: (0, 0), pipeline_mode=pl.Buffered(1))
                    for a in side_w]
                 + [side_cols(b_in, off) for off in side_offsets]
                 + [slab(a) for a in to_cast],
        out_specs=[out_spec, out_spec, nat(n_side * SIDE_W)] + [slab(a) for a in to_cast],
        out_shape=[jax.ShapeDtypeStruct((bsz, dil, sub, w), jnp.bfloat16),
                   jax.ShapeDtypeStruct((bsz, dil, sub, w), jnp.float32),
                   jax.ShapeDtypeStruct((bsz, seq, n_side * SIDE_W), jnp.bfloat16)]
                  + [jax.ShapeDtypeStruct(a.shape, jnp.bfloat16) for a in to_cast],
        scratch_shapes=[pltpu.VMEM((3, HEADS_PER_GROUP, QBLK, KBLK), jnp.float32)],
        compiler_params=pltpu.CompilerParams(
            dimension_semantics=("arbitrary", "arbitrary", "arbitrary"),
            vmem_limit_bytes=VMEM_LIMIT),
        name=name,
    )(qkv, qkv, qkv, qkv, qkv, qkv, qkv, hb, *side_w, *([b_in] * n_side), *to_cast)


def _dwconv3_rows(u, prev_row, next_row, w_ref):
    n = u.shape[0]
    row = lax.broadcasted_iota(jnp.int32, u.shape, 0)
    u_dn = jnp.where(row == 0, prev_row, pltpu.roll(u, 1, axis=0))
    u_up = jnp.where(row == n - 1, next_row, pltpu.roll(u, n - 1, axis=0))
    return u_dn * w_ref[0:1, :] + u * w_ref[1:2, :] + u_up * w_ref[2:3, :]


def _to_natural_order(src_ref, slab_ref, tmp_ref):
    dil, n, w = src_ref.shape
    n_slabs = w // LANES
    if dil <= 4:
        for r in range(dil):
            blk = src_ref[r].astype(jnp.float32)
            for c in range(n_slabs):
                slab_ref[c, pl.ds(r, n, stride=dil), :] = blk[:, c * LANES:(c + 1) * LANES]
    else:
        inner = 4
        outer = dil // inner
        assert outer <= 4
        plane = n * outer
        for r in range(dil):
            a, b = divmod(r, inner)
            blk = src_ref[r].astype(jnp.float32)
            for c in range(n_slabs):
                tmp_ref[c, pl.ds(b * plane + a, n, stride=outer), :] = blk[:, c * LANES:(c + 1) * LANES]
        for b in range(inner):
            for c in range(n_slabs):
                slab_ref[c, pl.ds(b, plane, stride=inner), :] = tmp_ref[c, b * plane:(b + 1) * plane, :]
    return jnp.concatenate([slab_ref[c] for c in range(n_slabs)], axis=-1)


def _mix_kernel(h_ref, gb_ref, gc_ref, hn_ref, ga_ref, gbb_ref, gcp_ref, gcn_ref, hnp_ref, hnn_ref,
                o0_ref, o1_ref, o2_ref, s0_ref, s1_ref, s2_ref,
                cw_ref, wa_ref, wb_ref, wo_ref, bo_ref, g1_ref, b1_ref, out_ref,
                slab_o1, slab_s1, slab_o2, slab_s2, tmp_o, tmp_s, *, tm):
    i = pl.program_id(1)
    last = pl.num_programs(1) - 1
    f32, bf16 = jnp.float32, jnp.bfloat16
    dot = functools.partial(jnp.dot, preferred_element_type=f32)

    u = gc_ref[...].astype(f32) * hn_ref[...].astype(f32)
    u_prev = (gcp_ref[...].astype(f32) * hnp_ref[...].astype(f32))[BF16_ROWS - 1:BF16_ROWS, :]
    u_next = (gcn_ref[...].astype(f32) * hnn_ref[...].astype(f32))[0:1, :]
    u_prev = jnp.where(i > 0, u_prev, 0.0)
    u_next = jnp.where(i < last, u_next, 0.0)
    conv = _dwconv3_rows(u, u_prev, u_next, cw_ref)
    y_a = dot((gb_ref[...].astype(f32) * conv).astype(bf16), wa_ref[...])

    o0, s0 = o0_ref[0].astype(f32), s0_ref[0]
    o1, s1 = _to_natural_order(o1_ref, slab_o1, tmp_o), _to_natural_order(s1_ref, slab_s1, tmp_s)
    o2, s2 = _to_natural_order(o2_ref, slab_o2, tmp_o), _to_natural_order(s2_ref, slab_s2, tmp_s)
    m = jnp.maximum(jnp.maximum(s0, s1), s2)
    e0, e1, e2 = jnp.exp(s0 - m), jnp.exp(s1 - m), jnp.exp(s2 - m)
    comb = (e0 * o0 + e1 * o1 + e2 * o2) / (e0 + e1 + e2)
    y_b = dot(comb.astype(bf16), wb_ref[...])

    merged = (jax.nn.sigmoid(ga_ref[...].astype(f32)) * y_a
              + jax.nn.sigmoid(gbb_ref[...].astype(f32)) * y_b).astype(bf16)
    blocks = [slice(0, tm - tm // 4), slice(tm - tm // 4, tm)]
    mix = [dot(merged[rows, :], wo_ref[...]) for rows in blocks]
    for rows, part in zip(blocks, mix):
        out_ref[rows, :] = _layer_norm(ALPHA * h_ref[rows, :] + part + bo_ref[...],
                                       g1_ref[...], b1_ref[...])


def _mix(h, sides, outs, stats, weights, *, tm):
    bsz, seq, dm = h.shape
    nhalo = tm // BF16_ROWS
    nrow_blk = seq // BF16_ROWS
    tile = lambda c: pl.BlockSpec((None, tm, dm), lambda b, i: (b, i, c))
    before = lambda c: pl.BlockSpec((None, BF16_ROWS, dm),
                                    lambda b, i: (b, jnp.maximum(i * nhalo - 1, 0), c))
    after = lambda c: pl.BlockSpec((None, BF16_ROWS, dm),
                                   lambda b, i: (b, jnp.minimum((i + 1) * nhalo, nrow_blk - 1), c))
    att = lambda d: pl.BlockSpec((None, d, tm // d, ATT_GROUP_W), lambda b, i: (b, 0, i, 0))
    att_specs = [att(d) for _, d in GROUPS]
    bc, ha, gb2 = sides
    slab = pltpu.VMEM((ATT_GROUP_W // LANES, tm, LANES), jnp.float32)
    return pl.pallas_call(
        functools.partial(_mix_kernel, tm=tm),
        grid=(bsz, seq // tm),
        in_specs=[tile(0), tile(0), tile(1), tile(0), tile(1), tile(0),
                  before(1), after(1), before(0), after(0)]
                 + att_specs + att_specs + [_resident(a) for a in weights],
        out_specs=tile(0),
        out_shape=jax.ShapeDtypeStruct((bsz, seq, dm), jnp.float32),
        scratch_shapes=[slab] * 6,
        compiler_params=pltpu.CompilerParams(
            dimension_semantics=("parallel", "parallel"),
            vmem_limit_bytes=VMEM_LIMIT),
        name="mix",
    )(h, bc, bc, ha, ha, gb2, bc, bc, ha, ha, *outs, *stats, *weights)


def _ffn_kernel(h_ref, hp_ref, hn_ref, wu_ref, bu_ref, cw_ref, cb_ref, wd_ref, bd_ref,
                g_ref, b_ref, out_ref, f_ref, *, tm, sub, chunks):
    i = pl.program_id(1)
    last = pl.num_programs(1) - 1
    f32 = jnp.float32
    h = h_ref[...]
    lhs_all = jnp.concatenate([h, hp_ref[...], hn_ref[...]], axis=0).astype(jnp.bfloat16)
    lhs = lhs_all[0:tm, :]
    lo = 0
    for ck in chunks:
        cols = slice(lo, lo + ck)
        gcols = slice(D_FF + lo, D_FF + lo + ck)
        a_all = jnp.dot(lhs_all, wu_ref[:, cols], preferred_element_type=f32) + bu_ref[:, cols]
        gate = jnp.dot(lhs, wu_ref[:, gcols], preferred_element_type=f32) + bu_ref[:, gcols]
        a = a_all[0:tm, :]
        a_prev = jnp.where(i > 0, a_all[tm + F32_ROWS - 1:tm + F32_ROWS, :], 0.0)
        a_next = jnp.where(i < last, a_all[tm + F32_ROWS:tm + F32_ROWS + 1, :], 0.0)
        pre = _dwconv3_rows(a, a_prev, a_next, cw_ref.at[:, cols]) + cb_ref[:, cols]
        f = 0.5 * pre * (1.0 + lax.erf(pre * (2.0 ** -0.5))) * gate
        f_ref[:, cols] = f.astype(jnp.bfloat16)
        lo += ck
    blocks = [slice(0, tm - sub), slice(tm - sub, tm)]
    ffn = [jnp.dot(f_ref[rows, :], wd_ref[...], preferred_element_type=f32) for rows in blocks]
    for rows, part in zip(blocks, ffn):
        out_ref[rows, :] = _layer_norm(ALPHA * h[rows, :] + part + bd_ref[...], g_ref[...], b_ref[...])


def _ffn(h1, wu, bu, cw, cb, wd, bd, g, b, *, tm, sub, chunks):
    bsz, seq, dm = h1.shape
    assert sum(chunks) == D_FF
    nhalo = tm // F32_ROWS
    nrow_blk = seq // F32_ROWS
    weights = (wu, bu, cw, cb, wd, bd, g, b)
    return pl.pallas_call(
        functools.partial(_ffn_kernel, tm=tm, sub=sub, chunks=chunks),
        grid=(bsz, seq // tm),
        in_specs=[
            pl.BlockSpec((None, tm, dm), lambda bb, i: (bb, i, 0)),
            pl.BlockSpec((None, F32_ROWS, dm), lambda bb, i: (bb, jnp.maximum(i * nhalo - 1, 0), 0)),
            pl.BlockSpec((None, F32_ROWS, dm),
                         lambda bb, i: (bb, jnp.minimum((i + 1) * nhalo, nrow_blk - 1), 0)),
        ] + [_resident(a) for a in weights],
        out_specs=pl.BlockSpec((None, tm, dm), lambda bb, i: (bb, i, 0)),
        out_shape=jax.ShapeDtypeStruct((bsz, seq, dm), jnp.float32),
        scratch_shapes=[pltpu.VMEM((tm, D_FF), jnp.bfloat16)],
        compiler_params=pltpu.CompilerParams(
            dimension_semantics=("parallel", "parallel"),
            vmem_limit_bytes=VMEM_LIMIT),
        name="ffn",
    )(h1, h1, h1, *weights)


def kernel(x, ln0_g, ln0_b, w_in, b_in, conv_w, w_a, w_b, w_o, b_o, ln1_g, ln1_b,
           w_up, b_up, ffn_conv_w, ffn_conv_b, w_down, b_down, ln2_g, ln2_b):
    bf16 = jnp.bfloat16
    row = lambda v: v.reshape(1, -1)
    slopes = _alibi_slopes()
    g0, b0 = row(ln0_g), row(ln0_b)
    bsz, seq, _ = x.shape
    l = 0
    bi = row(b_in[l])
    side_cols = ((OFF_B, OFF_C), (OFF_H, OFF_GA), (OFF_GB,))
    side_offsets = [[off + k * SIDE_W for off in cols for k in range(D_MODEL // SIDE_W)]
                    for cols in side_cols]
    all_side = [off for offs in side_offsets for off in offs]
    w_qkv = w_in[l][:, OFF_Q:OFF_GA].astype(bf16)
    proj = _qkv_proj(x, g0, b0, w_qkv, bi, w_in[l], all_side, tm=1024, sub=256)
    *qkvs, h, hb = proj[:N_GROUPS + 2]
    side_w = dict(zip(all_side, proj[N_GROUPS + 2:]))
    to_cast = ((w_up[l],), (w_down[l], w_a[l]), (w_b[l], w_o[l]))
    outs, stats, sides, cast = [], [], [], []
    for g, (_, dil) in enumerate(GROUPS):
        tq = min(ATTN_ROWS_PER_STEP, seq // dil)
        offsets = side_offsets[g]
        o, st, side, *casted = _attention(qkvs[g], hb, [side_w[off] for off in offsets], bi, offsets,
                                          to_cast[g], dil=dil, slopes=slopes[g], tq=tq,
                                          nres=ATTN_ROWS_PER_STEP // tq, name=f"attn{g}")
        outs.append(o)
        stats.append(st)
        sides.append(side)
        cast.extend(casted)
    w_up_b, w_down_b, w_a_b, w_b_b, w_o_b = cast
    mix_weights = (conv_w[l], w_a_b, w_b_b, w_o_b, row(b_o[l]), row(ln1_g[l]), row(ln1_b[l]))
    h1 = _mix(h, sides, outs, stats, mix_weights, tm=512)
    return _ffn(h1, w_up_b, row(b_up[l]), ffn_conv_w[l], row(ffn_conv_b[l]),
                w_down_b, row(b_down[l]), row(ln2_g[l]), row(ln2_b[l]),
                tm=1024, sub=128, chunks=_chunks(D_FF, 2 * MXU_TILE))
```

```python
import functools

import numpy as np
import jax
import jax.numpy as jnp
from jax import lax
from jax.experimental import pallas as pl
from jax.experimental.pallas import tpu as pltpu

D_MODEL = 1024
D_CONV = D_MODEL
HEAD_DIM = 64
HEADS_PER_GROUP = 8
GROUPS = ((128, 1), (512, 4), (2048, 16))
N_GROUPS = len(GROUPS)
REGROUPED = tuple(g for g, (_, d) in enumerate(GROUPS) if d > 1)
N_ATT_HEADS = N_GROUPS * HEADS_PER_GROUP
ATT_GROUP_W = HEADS_PER_GROUP * HEAD_DIM
ATT_QKV_W = N_GROUPS * ATT_GROUP_W
D_FF = 2816
LN_EPS = 1e-5
DEPTH = 1
assert DEPTH == 1
ALPHA = (2.0 * DEPTH) ** 0.25

OFF_B = 0
OFF_C = OFF_B + D_CONV
OFF_H = OFF_C + D_CONV
OFF_Q = OFF_H + D_CONV
OFF_K = OFF_Q + ATT_QKV_W
OFF_V = OFF_K + ATT_QKV_W
OFF_GA = OFF_V + ATT_QKV_W
OFF_GB = OFF_GA + D_MODEL

RADIUS = 64
QBLK = 2 * RADIUS
KBLK = QBLK + 2 * RADIUS
LANES = 128
BF16_ROWS = 16
F32_ROWS = 8
HEADS_PER_DOT = 4
DOT_W = HEADS_PER_DOT * HEAD_DIM
MXU_TILE = 256
SIDE_W = 512
ATTN_ROWS_PER_STEP = 1024
VMEM_LIMIT = 60000 * 1024

assert all(w // (2 * d) == RADIUS for w, d in GROUPS)
assert D_FF % MXU_TILE == 0


def _chunks(total, width):
    full = [width] * (total // width)
    rem = total - sum(full)
    return tuple(full + ([rem] if rem else []))


def _layer_norm(xf, g, b):
    mu = jnp.mean(xf, -1, keepdims=True)
    xc = xf - mu
    var = jnp.mean(xc * xc, -1, keepdims=True)
    return xc * lax.rsqrt(var + LN_EPS) * g + b


def _alibi_slopes():
    n = N_ATT_HEADS
    return np.exp2(-8.0 * np.arange(1, n + 1, dtype=np.float64) / n).reshape(N_GROUPS, HEADS_PER_GROUP)


def _resident(a):
    return pl.BlockSpec(a.shape, lambda b, i: (0,) * a.ndim, pipeline_mode=pl.Buffered(1))


def _resident_cols(a, offset, width):
    assert offset % width == 0
    return pl.BlockSpec((a.shape[0], width), lambda b, i: (0, offset // width),
                        pipeline_mode=pl.Buffered(1))


def _qkv_proj_kernel(x_ref, g_ref, b_ref, *refs, tm, sub, n_cast):
    n_w = 3 * N_GROUPS
    w_refs, bias_refs = refs[:n_w], refs[n_w:2 * n_w]
    cast_in = refs[2 * n_w:2 * n_w + n_cast]
    o_refs = refs[2 * n_w + n_cast:2 * n_w + n_cast + N_GROUPS]
    h_ref, hb_out_ref = refs[2 * n_w + n_cast + N_GROUPS:2 * n_w + n_cast + N_GROUPS + 2]
    cast_out = refs[2 * n_w + n_cast + N_GROUPS + 2:2 * n_w + 2 * n_cast + N_GROUPS + 2]
    hb_ref, slab_ref = refs[-2:]
    for src, dst in zip(cast_in, cast_out):
        dst[...] = src[...].astype(dst.dtype)
    n_slabs = D_MODEL // LANES
    for s in range(tm // sub):
        rows = slice(s * sub, (s + 1) * sub)
        buf = s % hb_ref.shape[0]
        h = _layer_norm(x_ref[rows, :], g_ref[...], b_ref[...])
        hb = h.astype(jnp.bfloat16)
        h_ref[rows, :] = h
        hb_out_ref[rows, :] = hb
        for c in range(n_slabs):
            slab_ref[buf, c] = h[:, c * LANES:(c + 1) * LANES]
        for g, (_, dil) in enumerate(GROUPS):
            n = sub // dil
            if dil == 1:
                lhs = hb
            else:
                slot = REGROUPED.index(g)
                for r in range(dil):
                    for c in range(n_slabs):
                        hb_ref[buf, slot, r * n:(r + 1) * n, c * LANES:(c + 1) * LANES] = (
                            slab_ref[buf, c, pl.ds(r, n, stride=dil), :].astype(jnp.bfloat16))
                lhs = hb_ref[buf, slot]
            for t in range(3):
                acc = (jnp.dot(lhs, w_refs[3 * g + t][...], preferred_element_type=jnp.float32)
                       + bias_refs[3 * g + t][...])
                if t == 0:
                    acc = acc * HEAD_DIM ** -0.5
                cols = slice(t * ATT_GROUP_W, (t + 1) * ATT_GROUP_W)
                for r in range(dil):
                    o_refs[g][r, s * n:(s + 1) * n, cols] = (
                        acc[r * n:(r + 1) * n, :].astype(o_refs[g].dtype))


def _qkv_proj(x, ln_g, ln_b, w_qkv, b_in, w_in_f32, cast_offsets, *, tm, sub):
    bsz, seq, dm = x.shape
    n = 3 * ATT_GROUP_W
    offsets = [off + g * ATT_GROUP_W for g in range(N_GROUPS) for off in (OFF_Q, OFF_K, OFF_V)]
    tile = pl.BlockSpec((None, tm, dm), lambda b, i: (b, i, 0))
    steps_i = seq // tm
    n_steps = bsz * steps_i
    slab_rows = dm // n_steps
    assert slab_rows * n_steps == dm and slab_rows % BF16_ROWS == 0
    assert all(off % SIDE_W == 0 for off in cast_offsets)
    slab_in = lambda off: pl.BlockSpec((slab_rows, SIDE_W), lambda b, i: (b * steps_i + i, off // SIDE_W))
    slab_out = pl.BlockSpec((slab_rows, SIDE_W), lambda b, i: (b * steps_i + i, 0))
    return pl.pallas_call(
        functools.partial(_qkv_proj_kernel, tm=tm, sub=sub, n_cast=len(cast_offsets)),
        grid=(bsz, steps_i),
        in_specs=[tile, _resident(ln_g), _resident(ln_b)]
                 + [_resident_cols(w_qkv, off - OFF_Q, ATT_GROUP_W) for off in offsets]
                 + [_resident_cols(b_in, off, ATT_GROUP_W) for off in offsets]
                 + [slab_in(off) for off in cast_offsets],
        out_specs=[pl.BlockSpec((None, d, tm // d, n), lambda b, i: (b, 0, i, 0)) for _, d in GROUPS]
                  + [tile, tile] + [slab_out] * len(cast_offsets),
        out_shape=[jax.ShapeDtypeStruct((bsz, d, seq // d, n), jnp.bfloat16) for _, d in GROUPS]
                  + [jax.ShapeDtypeStruct((bsz, seq, dm), jnp.float32),
                     jax.ShapeDtypeStruct((bsz, seq, dm), jnp.bfloat16)]
                  + [jax.ShapeDtypeStruct((dm, SIDE_W), jnp.bfloat16)] * len(cast_offsets),
        scratch_shapes=[pltpu.VMEM((2, len(REGROUPED), sub, dm), jnp.bfloat16),
                        pltpu.VMEM((2, dm // LANES, sub, LANES), jnp.float32)],
        compiler_params=pltpu.CompilerParams(
            dimension_semantics=("arbitrary", "arbitrary"),
            vmem_limit_bytes=VMEM_LIMIT),
        name="proj_qkv",
    )(x, ln_g, ln_b, *([w_qkv] * len(offsets)), *([b_in] * len(offsets)),
      *([w_in_f32] * len(cast_offsets)))


def _attn_kernel(q_ref, kp_ref, kc_ref, kn_ref, vp_ref, vc_ref, vn_ref, hb_ref, *refs,
                 dil, slopes, sub_len, tq, nres, n_side, n_cast):
    w_side, b_side = refs[:n_side], refs[n_side:2 * n_side]
    cast_in = refs[2 * n_side:2 * n_side + n_cast]
    o_ref, st_ref, side_ref = refs[2 * n_side + n_cast:2 * n_side + n_cast + 3]
    cast_out = refs[2 * n_side + n_cast + 3:2 * n_side + 2 * n_cast + 3]
    bias_ref = refs[-1]
    for src, dst in zip(cast_in, cast_out):
        dst[...] = src[...].astype(dst.dtype)
    side_w = w_side[0].shape[1]
    i = pl.program_id(2)
    nq = tq // QBLK

    def window(prev_ref, cur_ref, next_ref, rr, j, lanes):
        lo, hi = j * QBLK - RADIUS, (j + 1) * QBLK + RADIUS
        parts = []
        if lo < 0:
            parts.append(prev_ref[rr, :, lanes])
        parts.append(cur_ref[rr, max(lo, 0):min(hi, tq), lanes])
        if hi > tq:
            parts.append(next_ref[rr, :, lanes])
        return parts[0] if len(parts) == 1 else jnp.concatenate(parts, axis=0)

    @pl.when((pl.program_id(0) == 0) & (pl.program_id(1) == 0) & (i == 0))
    def _():
        row = lax.broadcasted_iota(jnp.int32, (QBLK, KBLK), 0)
        col = lax.broadcasted_iota(jnp.int32, (QBLK, KBLK), 1)
        rel = jnp.abs(col - RADIUS - row)
        band = rel <= RADIUS
        dist = (rel * dil).astype(jnp.float32)
        for var, ok in enumerate((band, band & (col >= RADIUS), band & (col < KBLK - RADIUS))):
            for h in range(HEADS_PER_GROUP):
                bias_ref[var, h] = jnp.where(ok, -slopes[h] * dist, -jnp.inf)

    head_of_lane = lax.broadcasted_iota(jnp.int32, (1, DOT_W), 1) // HEAD_DIM
    f32 = jnp.float32
    last_blk = sub_len // QBLK - 1

    n_parts = HEADS_PER_GROUP // HEADS_PER_DOT
    blocks = [(rr, j) for rr in range(nres) for j in range(nq)]
    scores = {}
    for rr, j in blocks:
        rows = slice(j * QBLK, (j + 1) * QBLK)
        for part in range(n_parts):
            lanes = slice(part * DOT_W, (part + 1) * DOT_W)
            qb = q_ref[rr, rows, lanes]
            kb = window(kp_ref, kc_ref, kn_ref, rr, j, lanes)
            q_heads = jnp.concatenate(
                [jnp.where(head_of_lane == hh, qb, jnp.zeros_like(qb)) for hh in range(HEADS_PER_DOT)],
                axis=0)
            scores[rr, j, part] = lax.dot_general(q_heads, kb, (((1,), (1,)), ((), ())),
                                                  preferred_element_type=f32)
    for k in range(n_side * side_w // MXU_TILE):
        blk_w, lo = divmod(k * MXU_TILE, side_w)
        cols = slice(lo, lo + MXU_TILE)
        acc = (jnp.dot(hb_ref[...], w_side[blk_w][:, cols], preferred_element_type=f32)
               + b_side[blk_w][:, cols])
        side_ref[:, k * MXU_TILE:(k + 1) * MXU_TILE] = acc.astype(side_ref.dtype)
    for rr, j in blocks:
        rows = slice(j * QBLK, (j + 1) * QBLK)
        blk = i * nq + j
        var = jnp.where(blk == 0, 1, jnp.where(blk == last_blk, 2, 0))
        for part in range(n_parts):
            lanes = slice(part * DOT_W, (part + 1) * DOT_W)
            vb = window(vp_ref, vc_ref, vn_ref, rr, j, lanes)
            s_all = scores[rr, j, part]
            p_heads, inv_den, lse = [], [], []
            for hh in range(HEADS_PER_DOT):
                s = s_all[hh * QBLK:(hh + 1) * QBLK, :] + bias_ref[var, part * HEADS_PER_DOT + hh]
                m = jnp.max(s, -1, keepdims=True)
                p = jnp.exp(s - m)
                den = jnp.sum(p, -1, keepdims=True)
                p_heads.append(p.astype(jnp.bfloat16))
                inv_den.append(1.0 / den)
                lse.append(m + jnp.log(den))
            o = jnp.zeros((QBLK, DOT_W), f32)
            st = jnp.zeros((QBLK, DOT_W), f32)
            for hh in range(HEADS_PER_DOT):
                sel = head_of_lane == hh
                o_hh = jnp.dot(p_heads[hh], vb, preferred_element_type=f32)
                o = jnp.where(sel, o_hh * inv_den[hh], o)
                st = jnp.where(sel, lse[hh], st)
            o_ref[rr, rows, lanes] = o.astype(o_ref.dtype)
            st_ref[rr, rows, lanes] = st


def _attention(qkv, hb, side_w, b_in, side_offsets, to_cast, *, dil, slopes, tq, nres, name):
    bsz, _, sub, _ = qkv.shape
    seq, dm = hb.shape[1:]
    w = ATT_GROUP_W
    nq = tq // RADIUS
    nblk = sub // RADIUS
    n_side = len(side_offsets)
    steps_i = sub // tq
    steps_r = dil // nres
    n_steps = bsz * steps_r * steps_i
    rows_nat = nres * tq
    assert steps_r * steps_i * rows_nat == seq
    assert nres * (tq // QBLK) * MXU_TILE >= n_side * SIDE_W
    assert all(a.shape[0] % (n_steps * BF16_ROWS) == 0 for a in to_cast)
    slab = lambda a: pl.BlockSpec((a.shape[0] // n_steps, a.shape[1]),
                                  lambda b, r, i: ((b * steps_r + r) * steps_i + i, 0))
    cur = lambda c: pl.BlockSpec((None, nres, tq, w), lambda b, r, i: (b, r, i, c))
    prev = lambda c: pl.BlockSpec((None, nres, RADIUS, w),
                                  lambda b, r, i: (b, r, jnp.maximum(i * nq - 1, 0), c))
    nxt = lambda c: pl.BlockSpec((None, nres, RADIUS, w),
                                 lambda b, r, i: (b, r, jnp.minimum((i + 1) * nq, nblk - 1), c))
    out_spec = pl.BlockSpec((None, nres, tq, w), lambda b, r, i: (b, r, i, 0))
    nat = lambda width: pl.BlockSpec((None, rows_nat, width), lambda b, r, i: (b, r * steps_i + i, 0))
    side_cols = lambda a, off: pl.BlockSpec((a.shape[0], SIDE_W), lambda b, r, i: (0, off // SIDE_W),
                                            pipeline_mode=pl.Buffered(1))
    assert all(off % SIDE_W == 0 for off in side_offsets)
    return pl.pallas_call(
        functools.partial(_attn_kernel, dil=dil, slopes=tuple(float(s) for s in slopes),
                          sub_len=sub, tq=tq, nres=nres, n_side=n_side, n_cast=len(to_cast)),
        grid=(bsz, steps_r, steps_i),
        in_specs=[cur(0), prev(1), cur(1), nxt(1), prev(2), cur(2), nxt(2), nat(dm)]
                 + [pl.BlockSpec(a.shape, lambda b, r, i: (0, 0), pipeline_mode=pl.Buffered(1))
                    for a in side_w]
                 + [side_cols(b_in, off) for off in side_offsets]
                 + [slab(a) for a in to_cast],
        out_specs=[out_spec, out_spec, nat(n_side * SIDE_W)] + [slab(a) for a in to_cast],
        out_shape=[jax.ShapeDtypeStruct((bsz, dil, sub, w), jnp.bfloat16),
                   jax.ShapeDtypeStruct((bsz, dil, sub, w), jnp.float32),
                   jax.ShapeDtypeStruct((bsz, seq, n_side * SIDE_W), jnp.bfloat16)]
                  + [jax.ShapeDtypeStruct(a.shape, jnp.bfloat16) for a in to_cast],
        scratch_shapes=[pltpu.VMEM((3, HEADS_PER_GROUP, QBLK, KBLK), jnp.float32)],
        compiler_params=pltpu.CompilerParams(
            dimension_semantics=("arbitrary", "arbitrary", "arbitrary"),
            vmem_limit_bytes=VMEM_LIMIT),
        name=name,
    )(qkv, qkv, qkv, qkv, qkv, qkv, qkv, hb, *side_w, *([b_in] * n_side), *to_cast)


def _dwconv3_rows(u, prev_row, next_row, w_ref):
    n = u.shape[0]
    row = lax.broadcasted_iota(jnp.int32, u.shape, 0)
    u_dn = jnp.where(row == 0, prev_row, pltpu.roll(u, 1, axis=0))
    u_up = jnp.where(row == n - 1, next_row, pltpu.roll(u, n - 1, axis=0))
    return u_dn * w_ref[0:1, :] + u * w_ref[1:2, :] + u_up * w_ref[2:3, :]


def _to_natural_order(src_ref, slab_ref, tmp_ref):
    dil, n, w = src_ref.shape
    n_slabs = w // LANES
    if dil <= 4:
        for r in range(dil):
            blk = src_ref[r].astype(jnp.float32)
            for c in range(n_slabs):
                slab_ref[c, pl.ds(r, n, stride=dil), :] = blk[:, c * LANES:(c + 1) * LANES]
    else:
        inner = 4
        outer = dil // inner
        assert outer <= 4
        plane = n * outer
        for r in range(dil):
            a, b = divmod(r, inner)
            blk = src_ref[r].astype(jnp.float32)
            for c in range(n_slabs):
                tmp_ref[c, pl.ds(b * plane + a, n, stride=outer), :] = blk[:, c * LANES:(c + 1) * LANES]
        for b in range(inner):
            for c in range(n_slabs):
                slab_ref[c, pl.ds(b, plane, stride=inner), :] = tmp_ref[c, b * plane:(b + 1) * plane, :]
    return jnp.concatenate([slab_ref[c] for c in range(n_slabs)], axis=-1)


def _mix_kernel(h_ref, gb_ref, gc_ref, hn_ref, ga_ref, gbb_ref, gcp_ref, gcn_ref, hnp_ref, hnn_ref,
                o0_ref, o1_ref, o2_ref, s0_ref, s1_ref, s2_ref,
                cw_ref, wa_ref, wb_ref, wo_ref, bo_ref, g1_ref, b1_ref, out_ref,
                slab_o1, slab_s1, slab_o2, slab_s2, tmp_o, tmp_s, *, tm):
    i = pl.program_id(1)
    last = pl.num_programs(1) - 1
    f32, bf16 = jnp.float32, jnp.bfloat16
    dot = functools.partial(jnp.dot, preferred_element_type=f32)

    u = gc_ref[...].astype(f32) * hn_ref[...].astype(f32)
    u_prev = (gcp_ref[...].astype(f32) * hnp_ref[...].astype(f32))[BF16_ROWS - 1:BF16_ROWS, :]
    u_next = (gcn_ref[...].astype(f32) * hnn_ref[...].astype(f32))[0:1, :]
    u_prev = jnp.where(i > 0, u_prev, 0.0)
    u_next = jnp.where(i < last, u_next, 0.0)
    conv = _dwconv3_rows(u, u_prev, u_next, cw_ref)
    y_a = dot((gb_ref[...].astype(f32) * conv).astype(bf16), wa_ref[...])

    o0, s0 = o0_ref[0].astype(f32), s0_ref[0]
    o1, s1 = _to_natural_order(o1_ref, slab_o1, tmp_o), _to_natural_order(s1_ref, slab_s1, tmp_s)
    o2, s2 = _to_natural_order(o2_ref, slab_o2, tmp_o), _to_natural_order(s2_ref, slab_s2, tmp_s)
    y_b = []
    for rows in (slice(0, tm // 2), slice(tm // 2, tm)):
        t0, t1, t2 = s0[rows, :], s1[rows, :], s2[rows, :]
        m = jnp.maximum(jnp.maximum(t0, t1), t2)
        e0, e1, e2 = jnp.exp(t0 - m), jnp.exp(t1 - m), jnp.exp(t2 - m)
        comb = (e0 * o0[rows, :] + e1 * o1[rows, :] + e2 * o2[rows, :]) / (e0 + e1 + e2)
        y_b.append(dot(comb.astype(bf16), wb_ref[...]))
    y_b = jnp.concatenate(y_b, axis=0)

    merged = (jax.nn.sigmoid(ga_ref[...].astype(f32)) * y_a
              + jax.nn.sigmoid(gbb_ref[...].astype(f32)) * y_b).astype(bf16)
    blocks = [slice(s * (tm // 2), (s + 1) * (tm // 2)) for s in range(2)]
    mix = [dot(merged[rows, :], wo_ref[...]) for rows in blocks]
    for rows, part in zip(blocks, mix):
        out_ref[rows, :] = _layer_norm(ALPHA * h_ref[rows, :] + part + bo_ref[...],
                                       g1_ref[...], b1_ref[...])


def _mix(h, sides, outs, stats, weights, *, tm):
    bsz, seq, dm = h.shape
    nhalo = tm // BF16_ROWS
    nrow_blk = seq // BF16_ROWS
    tile = lambda c: pl.BlockSpec((None, tm, dm), lambda b, i: (b, i, c))
    before = lambda c: pl.BlockSpec((None, BF16_ROWS, dm),
                                    lambda b, i: (b, jnp.maximum(i * nhalo - 1, 0), c))
    after = lambda c: pl.BlockSpec((None, BF16_ROWS, dm),
                                   lambda b, i: (b, jnp.minimum((i + 1) * nhalo, nrow_blk - 1), c))
    att = lambda d: pl.BlockSpec((None, d, tm // d, ATT_GROUP_W), lambda b, i: (b, 0, i, 0))
    att_specs = [att(d) for _, d in GROUPS]
    bc, ha, gb2 = sides
    slab = pltpu.VMEM((ATT_GROUP_W // LANES, tm, LANES), jnp.float32)
    return pl.pallas_call(
        functools.partial(_mix_kernel, tm=tm),
        grid=(bsz, seq // tm),
        in_specs=[tile(0), tile(0), tile(1), tile(0), tile(1), tile(0),
                  before(1), after(1), before(0), after(0)]
                 + att_specs + att_specs + [_resident(a) for a in weights],
        out_specs=tile(0),
        out_shape=jax.ShapeDtypeStruct((bsz, seq, dm), jnp.float32),
        scratch_shapes=[slab] * 6,
        compiler_params=pltpu.CompilerParams(
            dimension_semantics=("parallel", "parallel"),
            vmem_limit_bytes=VMEM_LIMIT),
        name="mix",
    )(h, bc, bc, ha, ha, gb2, bc, bc, ha, ha, *outs, *stats, *weights)


def _ffn_kernel(h_ref, hp_ref, hn_ref, wu_ref, bu_ref, cw_ref, cb_ref, wd_ref, bd_ref,
                g_ref, b_ref, out_ref, f_ref, *, tm, sub, chunks):
    i = pl.program_id(1)
    last = pl.num_programs(1) - 1
    f32 = jnp.float32
    h = h_ref[...]
    lhs_all = jnp.concatenate([h, hp_ref[...], hn_ref[...]], axis=0).astype(jnp.bfloat16)
    lhs = lhs_all[0:tm, :]
    lo = 0
    for ck in chunks:
        cols = slice(lo, lo + ck)
        gcols = slice(D_FF + lo, D_FF + lo + ck)
        a_all = jnp.dot(lhs_all, wu_ref[:, cols], preferred_element_type=f32) + bu_ref[:, cols]
        gate = jnp.dot(lhs, wu_ref[:, gcols], preferred_element_type=f32) + bu_ref[:, gcols]
        a = a_all[0:tm, :]
        a_prev = jnp.where(i > 0, a_all[tm + F32_ROWS - 1:tm + F32_ROWS, :], 0.0)
        a_next = jnp.where(i < last, a_all[tm + F32_ROWS:tm + F32_ROWS + 1, :], 0.0)
        pre = _dwconv3_rows(a, a_prev, a_next, cw_ref.at[:, cols]) + cb_ref[:, cols]
        f = 0.5 * pre * (1.0 + lax.erf(pre * (2.0 ** -0.5))) * gate
        f_ref[:, cols] = f.astype(jnp.bfloat16)
        lo += ck
    blocks = [slice(0, tm - sub), slice(tm - sub, tm)]
    ffn = [jnp.dot(f_ref[rows, :], wd_ref[...], preferred_element_type=f32) for rows in blocks]
    for rows, part in zip(blocks, ffn):
        out_ref[rows, :] = _layer_norm(ALPHA * h[rows, :] + part + bd_ref[...], g_ref[...], b_ref[...])


def _ffn(h1, wu, bu, cw, cb, wd, bd, g, b, *, tm, sub, chunks):
    bsz, seq, dm = h1.shape
    assert sum(chunks) == D_FF
    nhalo = tm // F32_ROWS
    nrow_blk = seq // F32_ROWS
    weights = (wu, bu, cw, cb, wd, bd, g, b)
    return pl.pallas_call(
        functools.partial(_ffn_kernel, tm=tm, sub=sub, chunks=chunks),
        grid=(bsz, seq // tm),
        in_specs=[
            pl.BlockSpec((None, tm, dm), lambda bb, i: (bb, i, 0)),
            pl.BlockSpec((None, F32_ROWS, dm), lambda bb, i: (bb, jnp.maximum(i * nhalo - 1, 0), 0)),
            pl.BlockSpec((None, F32_ROWS, dm),
                         lambda bb, i: (bb, jnp.minimum((i + 1) * nhalo, nrow_blk - 1), 0)),
        ] + [_resident(a) for a in weights],
        out_specs=pl.BlockSpec((None, tm, dm), lambda bb, i: (bb, i, 0)),
        out_shape=jax.ShapeDtypeStruct((bsz, seq, dm), jnp.float32),
        scratch_shapes=[pltpu.VMEM((tm, D_FF), jnp.bfloat16)],
        compiler_params=pltpu.CompilerParams(
            dimension_semantics=("parallel", "parallel"),
            vmem_limit_bytes=VMEM_LIMIT),
        name="ffn",
    )(h1, h1, h1, *weights)


def kernel(x, ln0_g, ln0_b, w_in, b_in, conv_w, w_a, w_b, w_o, b_o, ln1_g, ln1_b,
           w_up, b_up, ffn_conv_w, ffn_conv_b, w_down, b_down, ln2_g, ln2_b):
    bf16 = jnp.bfloat16
    row = lambda v: v.reshape(1, -1)
    slopes = _alibi_slopes()
    g0, b0 = row(ln0_g), row(ln0_b)
    bsz, seq, _ = x.shape
    l = 0
    bi = row(b_in[l])
    side_cols = ((OFF_B, OFF_C), (OFF_H, OFF_GA), (OFF_GB,))
    side_offsets = [[off + k * SIDE_W for off in cols for k in range(D_MODEL // SIDE_W)]
                    for cols in side_cols]
    all_side = [off for offs in side_offsets for off in offs]
    w_qkv = w_in[l][:, OFF_Q:OFF_GA].astype(bf16)
    proj = _qkv_proj(x, g0, b0, w_qkv, bi, w_in[l], all_side, tm=1024, sub=256)
    *qkvs, h, hb = proj[:N_GROUPS + 2]
    side_w = dict(zip(all_side, proj[N_GROUPS + 2:]))
    to_cast = ((w_up[l],), (w_down[l], w_a[l]), (w_b[l], w_o[l]))
    outs, stats, sides, cast = [], [], [], []
    for g, (_, dil) in enumerate(GROUPS):
        tq = min(ATTN_ROWS_PER_STEP, seq // dil)
        offsets = side_offsets[g]
        o, st, side, *casted = _attention(qkvs[g], hb, [side_w[off] for off in offsets], bi, offsets,
                                          to_cast[g], dil=dil, slopes=slopes[g], tq=tq,
                                          nres=ATTN_ROWS_PER_STEP // tq, name=f"attn{g}")
        outs.append(o)
        stats.append(st)
        sides.append(side)
        cast.extend(casted)
    w_up_b, w_down_b, w_a_b, w_b_b, w_o_b = cast
    mix_weights = (conv_w[l], w_a_b, w_b_b, w_o_b, row(b_o[l]), row(ln1_g[l]), row(ln1_b[l]))
    h1 = _mix(h, sides, outs, stats, mix_weights, tm=512)
    return _ffn(h1, w_up_b, row(b_up[l]), ffn_conv_w[l], row(ffn_conv_b[l]),
                w_down_b, row(b_down[l]), row(ln2_g[l]), row(ln2_b[l]),
                tm=1024, sub=256, chunks=_chunks(D_FF, 2 * MXU_TILE))
```
